```python
import math, functools
import jax, jax.numpy as jnp
from jax import lax
import numpy as np


D_MODEL = 2048
BATCH = 2
SEQ = 4096
DEPTH = 2
DEC_BATCH = 128
DEC_SEQ = 4
PAST_LEN = 2048
PAGE_SIZE = 128

D_MIX = D_MODEL
D_SSM = D_MIX // 2
D_ATTN = D_MIX - D_SSM
HEAD_DIM = 128
N_HEADS = D_ATTN // HEAD_DIM
SSM_GROUP = 16
SSM_GROUPS = D_SSM // SSM_GROUP
SSM_STATE = 64
D_IN = D_SSM + 3 * D_ATTN + N_HEADS
D_FF = 5504
Q_BLOCK = 128
EPS = 1e-6
DT_MIN = 1e-3
DT_MAX = 1e-1

kernel_name = "hymba_s5_fox_macaron_step"


def rms_norm(x, g):
    xf = x.astype(jnp.float32)
    y = xf * lax.rsqrt(jnp.mean(xf * xf, axis=-1, keepdims=True) + EPS)
    return (y * g.astype(jnp.float32)).astype(x.dtype)


def half_ffn(x, g_pre, g_post, w_gate, w_up, w_down):
    h = rms_norm(x, g_pre)
    f = (jax.nn.silu(h @ w_gate) * (h @ w_up)) @ w_down
    return x + 0.5 * rms_norm(f, g_post)


def forgetting_attention(q, k, v, c_q, c_k, q_pos, k_pos):
    s = jnp.einsum('nqhd,nkhd->nhqk', q, k).astype(jnp.float32) * (HEAD_DIM ** -0.5)
    s = s + jnp.transpose(c_q, (0, 2, 1))[:, :, :, None] - jnp.transpose(c_k, (0, 2, 1))[:, :, None, :]
    mask = k_pos[None, :] <= q_pos[:, None]
    s = jnp.where(mask[None, None], s, -jnp.inf)
    p = jax.nn.softmax(s, axis=-1).astype(v.dtype)
    return jnp.einsum('nhqk,nkhd->nqhd', p, v)


def prompt_attend(q, k, v, logf):
    n, L = q.shape[0], q.shape[1]
    nb = L // Q_BLOCK
    c = jnp.cumsum(logf, axis=1)
    pos = jnp.arange(L)
    qb = q.reshape(n, nb, Q_BLOCK, N_HEADS, HEAD_DIM).swapaxes(0, 1)
    cb = c.reshape(n, nb, Q_BLOCK, N_HEADS).swapaxes(0, 1)
    pb = pos.reshape(nb, Q_BLOCK)
    out = lax.map(lambda blk: forgetting_attention(blk[0], k, v, blk[1], c, blk[2], pos), (qb, cb, pb))
    return out.swapaxes(0, 1).reshape(n, L, N_HEADS, HEAD_DIM)


def sample_attend(q, k, v, logf, cache_k, cache_v, cache_logf, page_table, layer):
    n, L = q.shape[0], q.shape[1]
    n_past = page_table.shape[1] * cache_k.shape[2]
    k_past = cache_k[layer, page_table].reshape(n, n_past, N_HEADS, HEAD_DIM)
    v_past = cache_v[layer, page_table].reshape(n, n_past, N_HEADS, HEAD_DIM)
    lf_past = cache_logf[layer, page_table].reshape(n, n_past, N_HEADS).astype(jnp.float32)
    k_all = jnp.concatenate([k_past.astype(k.dtype), k], axis=1)
    v_all = jnp.concatenate([v_past.astype(v.dtype), v], axis=1)
    c_all = jnp.cumsum(jnp.concatenate([lf_past, logf], axis=1), axis=1)
    q_pos = n_past + jnp.arange(L)
    k_pos = jnp.arange(n_past + L)
    return forgetting_attention(q, k_all, v_all, c_all[:, n_past:], c_all, q_pos, k_pos)


def s5_scan(u, h0, a_re, a_im, log_dt, b_re, b_im, c_re, c_im, d):
    f32 = jnp.float32
    n, L, _ = u.shape
    a_re = a_re.astype(f32); a_im = a_im.astype(f32)
    b_re = b_re.astype(f32); b_im = b_im.astype(f32)
    dt = jnp.exp(log_dt.astype(f32))[:, None]
    mag = jnp.exp(dt * a_re)
    ab_re = mag * jnp.cos(dt * a_im)
    ab_im = mag * jnp.sin(dt * a_im)
    den = a_re * a_re + a_im * a_im
    f_re = ((ab_re - 1.0) * a_re + ab_im * a_im) / den
    f_im = (ab_im * a_re - (ab_re - 1.0) * a_im) / den
    bb_re = f_re[..., None] * b_re - f_im[..., None] * b_im
    bb_im = f_re[..., None] * b_im + f_im[..., None] * b_re
    uf = u.astype(f32)
    ug = uf.reshape(n, L, SSM_GROUPS, SSM_GROUP)
    x_re = jnp.einsum('nlgc,gpc->nlgp', ug, bb_re)
    x_im = jnp.einsum('nlgc,gpc->nlgp', ug, bb_im)
    h0r = h0[..., 0].astype(f32); h0i = h0[..., 1].astype(f32)
    x_re = x_re.at[:, 0].add(ab_re * h0r - ab_im * h0i)
    x_im = x_im.at[:, 0].add(ab_re * h0i + ab_im * h0r)
    A_re = jnp.broadcast_to(ab_re, x_re.shape)
    A_im = jnp.broadcast_to(ab_im, x_im.shape)

    def combine(e1, e2):
        a1r, a1i, b1r, b1i = e1
        a2r, a2i, b2r, b2i = e2
        return (a1r * a2r - a1i * a2i,
                a1r * a2i + a1i * a2r,
                a2r * b1r - a2i * b1i + b2r,
                a2r * b1i + a2i * b1r + b2i)

    _, _, h_re, h_im = lax.associative_scan(combine, (A_re, A_im, x_re, x_im), axis=1)
    y = (jnp.einsum('nlgp,gcp->nlgc', h_re, c_re.astype(f32))
         - jnp.einsum('nlgp,gcp->nlgc', h_im, c_im.astype(f32)))
    y = y.reshape(n, L, D_SSM) + d.astype(f32) * uf
    h_last = jnp.stack([h_re[:, -1], h_im[:, -1]], axis=-1)
    return y.astype(u.dtype), h_last


def token_mixing(h, ssm_h0, attend, p):
    n, L, _ = h.shape
    proj = h @ p['w_in']
    u = proj[..., :D_SSM]
    q = proj[..., D_SSM:D_SSM + D_ATTN].reshape(n, L, N_HEADS, HEAD_DIM)
    k = proj[..., D_SSM + D_ATTN:D_SSM + 2 * D_ATTN].reshape(n, L, N_HEADS, HEAD_DIM)
    v = proj[..., D_SSM + 2 * D_ATTN:D_SSM + 3 * D_ATTN].reshape(n, L, N_HEADS, HEAD_DIM)
    logf = jax.nn.log_sigmoid(proj[..., D_SSM + 3 * D_ATTN:].astype(jnp.float32)
                              + p['b_forget'].astype(jnp.float32))
    attn = attend(q, k, v, logf).reshape(n, L, D_ATTN)
    y, h_last = s5_scan(u, ssm_h0, p['a_re'], p['a_im'], p['log_dt'], p['b_re'], p['b_im'],
                        p['c_re'], p['c_im'], p['d'])
    y = jax.nn.gelu(y)
    y = y * jax.nn.sigmoid(y @ p['glu_w'] + p['glu_b'])
    merged = jnp.concatenate([rms_norm(attn, p['g_attn']), rms_norm(y, p['g_ssm'])], axis=-1)
    return merged @ p['w_out'], k, v, logf, h_last


def trunk_layer(x, ssm_h0, attend, p):
    ng = p['norm_g']
    x = half_ffn(x, ng[0], ng[1], p['ffn_gate'][0], p['ffn_up'][0], p['ffn_down'][0])
    m, k, v, logf, h_last = token_mixing(rms_norm(x, ng[2]), ssm_h0, attend, p)
    x = x + rms_norm(m, ng[3])
    x = half_ffn(x, ng[4], ng[5], p['ffn_gate'][1], p['ffn_up'][1], p['ffn_down'][1])
    return x, k, v, logf, h_last


def setup_inputs(seed: int = 0) -> dict:
    key = jax.random.key(seed)
    ks = jax.random.split(key, 32)
    f32 = jnp.float32
    n_pages = PAST_LEN // PAGE_SIZE
    n_used = DEC_BATCH * n_pages
    n_pool = (n_used * 5) // 4
    x_prompt = jax.random.normal(ks[0], (BATCH, SEQ, D_MODEL), f32)
    x_sample = jax.random.normal(ks[1], (DEC_BATCH, DEC_SEQ, D_MODEL), f32)
    cache_k = jax.random.normal(ks[2], (DEPTH, n_pool, PAGE_SIZE, N_HEADS, HEAD_DIM), f32)
    cache_v = jax.random.normal(ks[3], (DEPTH, n_pool, PAGE_SIZE, N_HEADS, HEAD_DIM), f32)
    cache_logf = jax.nn.log_sigmoid(3.0 + jax.random.normal(ks[4], (DEPTH, n_pool, PAGE_SIZE, N_HEADS), f32))
    state_ssm = 0.5 * jax.random.normal(ks[5], (DEPTH, DEC_BATCH, SSM_GROUPS, SSM_STATE, 2), f32)
    page_table = jax.random.permutation(ks[6], n_pool)[:n_used].reshape(DEC_BATCH, n_pages).astype(jnp.int32)
    norm_g = 1.0 + 0.05 * jax.random.normal(ks[7], (DEPTH, 6, D_MODEL), f32)
    ffn_gate = jax.random.normal(ks[8], (DEPTH, 2, D_MODEL, D_FF), f32) * D_MODEL ** -0.5
    ffn_up = jax.random.normal(ks[9], (DEPTH, 2, D_MODEL, D_FF), f32) * D_MODEL ** -0.5
    ffn_down = jax.random.normal(ks[10], (DEPTH, 2, D_FF, D_MODEL), f32) * D_FF ** -0.5
    w_in = jax.random.normal(ks[11], (DEPTH, D_MODEL, D_IN), f32) * D_MODEL ** -0.5
    b_forget = jax.random.uniform(ks[12], (DEPTH, N_HEADS), f32, minval=1.0, maxval=5.0)
    ssm_a_re = -0.5 + 0.01 * jax.random.normal(ks[13], (DEPTH, SSM_GROUPS, SSM_STATE), f32)
    ssm_a_im = (math.pi * jnp.arange(SSM_STATE, dtype=f32)
                + 0.01 * jax.random.normal(ks[14], (DEPTH, SSM_GROUPS, SSM_STATE), f32))
    ssm_log_dt = jax.random.uniform(ks[15], (DEPTH, SSM_GROUPS), f32,
                                    minval=math.log(DT_MIN), maxval=math.log(DT_MAX))
    ssm_b_re = jax.random.normal(ks[16], (DEPTH, SSM_GROUPS, SSM_STATE, SSM_GROUP), f32) * (2 * SSM_GROUP) ** -0.5
    ssm_b_im = jax.random.normal(ks[17], (DEPTH, SSM_GROUPS, SSM_STATE, SSM_GROUP), f32) * (2 * SSM_GROUP) ** -0.5
    ssm_c_re = jax.random.normal(ks[18], (DEPTH, SSM_GROUPS, SSM_GROUP, SSM_STATE), f32) * (2 * SSM_STATE) ** -0.5
    ssm_c_im = jax.random.normal(ks[19], (DEPTH, SSM_GROUPS, SSM_GROUP, SSM_STATE), f32) * (2 * SSM_STATE) ** -0.5
    ssm_d = jax.random.normal(ks[20], (DEPTH, D_SSM), f32)
    glu_w = jax.random.normal(ks[21], (DEPTH, D_SSM, D_SSM), f32) * D_SSM ** -0.5
    glu_b = 0.01 * jax.random.normal(ks[22], (DEPTH, D_SSM), f32)
    g_attn_out = 1.0 + 0.05 * jax.random.normal(ks[23], (DEPTH, D_ATTN), f32)
    g_ssm_out = 1.0 + 0.05 * jax.random.normal(ks[24], (DEPTH, D_SSM), f32)
    w_out = jax.random.normal(ks[25], (DEPTH, D_MIX, D_MODEL), f32) * D_MIX ** -0.5
    return {"x_prompt": x_prompt, "x_sample": x_sample,
            "cache_k": cache_k, "cache_v": cache_v, "cache_logf": cache_logf,
            "state_ssm": state_ssm, "page_table": page_table,
            "norm_g": norm_g, "ffn_gate": ffn_gate, "ffn_up": ffn_up, "ffn_down": ffn_down,
            "w_in": w_in, "b_forget": b_forget,
            "ssm_a_re": ssm_a_re, "ssm_a_im": ssm_a_im, "ssm_log_dt": ssm_log_dt,
            "ssm_b_re": ssm_b_re, "ssm_b_im": ssm_b_im, "ssm_c_re": ssm_c_re, "ssm_c_im": ssm_c_im,
            "ssm_d": ssm_d, "glu_w": glu_w, "glu_b": glu_b,
            "g_attn_out": g_attn_out, "g_ssm_out": g_ssm_out, "w_out": w_out}


def reference(x_prompt, x_sample, cache_k, cache_v, cache_logf, state_ssm, page_table,
              norm_g, ffn_gate, ffn_up, ffn_down, w_in, b_forget,
              ssm_a_re, ssm_a_im, ssm_log_dt, ssm_b_re, ssm_b_im, ssm_c_re, ssm_c_im,
              ssm_d, glu_w, glu_b, g_attn_out, g_ssm_out, w_out):
    xp, xs = x_prompt, x_sample
    kp, vp, lfp, sp = [], [], [], []
    kss, vss, lfs, sss = [], [], [], []
    h0_prompt = jnp.zeros((x_prompt.shape[0], SSM_GROUPS, SSM_STATE, 2), jnp.float32)
    for l in range(DEPTH):
        p = {'norm_g': norm_g[l], 'ffn_gate': ffn_gate[l], 'ffn_up': ffn_up[l], 'ffn_down': ffn_down[l],
             'w_in': w_in[l], 'b_forget': b_forget[l],
             'a_re': ssm_a_re[l], 'a_im': ssm_a_im[l], 'log_dt': ssm_log_dt[l],
             'b_re': ssm_b_re[l], 'b_im': ssm_b_im[l], 'c_re': ssm_c_re[l], 'c_im': ssm_c_im[l],
             'd': ssm_d[l], 'glu_w': glu_w[l], 'glu_b': glu_b[l],
             'g_attn': g_attn_out[l], 'g_ssm': g_ssm_out[l], 'w_out': w_out[l]}
        xp, k1, v1, lf1, s1 = trunk_layer(xp, h0_prompt, prompt_attend, p)
        attend_s = functools.partial(sample_attend, cache_k=cache_k, cache_v=cache_v,
                                     cache_logf=cache_logf, page_table=page_table, layer=l)
        xs, k2, v2, lf2, s2 = trunk_layer(xs, state_ssm[l], attend_s, p)
        kp.append(k1); vp.append(v1); lfp.append(lf1); sp.append(s1)
        kss.append(k2); vss.append(v2); lfs.append(lf2); sss.append(s2)
    return (xp, xs,
            jnp.stack(kp), jnp.stack(vp), jnp.stack(lfp), jnp.stack(sp),
            jnp.stack(kss), jnp.stack(vss), jnp.stack(lfs), jnp.stack(sss))
```

```python
import functools

import jax
import jax.numpy as jnp
from jax import lax
from jax.experimental import pallas as pl
from jax.experimental.pallas import tpu as pltpu

F32 = jnp.float32
BF16 = jnp.bfloat16

D_MODEL = 2048
D_SSM = 1024
D_ATTN = 1024
HEAD_DIM = 128
N_HEADS = 8
SSM_GROUP = 16
SSM_GROUPS = 64
SSM_STATE = 64
D_FF = 5504
EPS = 1e-6

LANES = 128
GROUPS_PER_BLOCK = LANES // SSM_GROUP
N_LANE_BLOCKS = D_SSM // LANES
STATE_COLS = GROUPS_PER_BLOCK * SSM_STATE
S5_CHUNK = 16
FF_TILE = 512
D_FF_PAD = ((D_FF + FF_TILE - 1) // FF_TILE) * FF_TILE
ROW_TILE = 512
ATT_TILE = 512
PAGES_PER_STEP = 4
NEW_KEY_PAD = 16
VMEM_LIMIT = 56 * 1024 * 1024
NEG_BIG = -1e30


def _cparams(sem):
    return pltpu.CompilerParams(dimension_semantics=sem, vmem_limit_bytes=VMEM_LIMIT)


def _rms(x, g):
    ms = jnp.mean(x * x, axis=-1, keepdims=True)
    return x * lax.rsqrt(ms + EPS) * g


def _log_sigmoid(z):
    return jnp.minimum(z, 0.0) - jnp.log1p(jnp.exp(-jnp.abs(z)))


def _ffn_kernel(x_ref, gpre_ref, gpost_ref, wg_ref, wu_ref, wd_ref, o_ref, h_ref, acc_ref):
    j = pl.program_id(1)

    @pl.when(j == 0)
    def _():
        h_ref[...] = _rms(x_ref[...], gpre_ref[...]).astype(BF16)
        acc_ref[...] = jnp.zeros_like(acc_ref)

    h = h_ref[...]
    g = jnp.dot(h, wg_ref[...], preferred_element_type=F32)
    u = jnp.dot(h, wu_ref[...], preferred_element_type=F32)
    a = (g * jax.nn.sigmoid(g) * u).astype(BF16)
    acc_ref[...] += jnp.dot(a, wd_ref[...], preferred_element_type=F32)

    @pl.when(j == pl.num_programs(1) - 1)
    def _():
        o_ref[...] = x_ref[...] + 0.5 * _rms(acc_ref[...], gpost_ref[...])


def _ffn_half(x, g_pre, g_post, wg, wu, wd):
    m = x.shape[0]
    tm = min(ROW_TILE, m)
    grid = (m // tm, D_FF_PAD // FF_TILE)
    return pl.pallas_call(
        _ffn_kernel,
        grid=grid,
        in_specs=[
            pl.BlockSpec((tm, D_MODEL), lambda i, j: (i, 0)),
            pl.BlockSpec((1, D_MODEL), lambda i, j: (0, 0)),
            pl.BlockSpec((1, D_MODEL), lambda i, j: (0, 0)),
            pl.BlockSpec((D_MODEL, FF_TILE), lambda i, j: (0, j)),
            pl.BlockSpec((D_MODEL, FF_TILE), lambda i, j: (0, j)),
            pl.BlockSpec((FF_TILE, D_MODEL), lambda i, j: (j, 0)),
        ],
        out_specs=pl.BlockSpec((tm, D_MODEL), lambda i, j: (i, 0)),
        out_shape=jax.ShapeDtypeStruct((m, D_MODEL), F32),
        scratch_shapes=[pltpu.VMEM((tm, D_MODEL), BF16), pltpu.VMEM((tm, D_MODEL), F32)],
        compiler_params=_cparams(("parallel", "arbitrary")),
        name="ffn_half",
    )(x, g_pre, g_post, wg, wu, wd)


def _mix_in_kernel(x_ref, g_ref, w_ref, wf_ref, bf_ref,
                   u_ref, q_ref, k_ref, v_ref, kb_ref, vb_ref, lf_ref, h_ref):
    j = pl.program_id(1)

    @pl.when(j == 0)
    def _():
        h = _rms(x_ref[...], g_ref[...]).astype(BF16)
        h_ref[...] = h
        z = jnp.dot(h, wf_ref[...], preferred_element_type=F32) + bf_ref[...]
        lf_ref[...] = _log_sigmoid(z)

    p = jnp.dot(h_ref[...], w_ref[...], preferred_element_type=F32)

    @pl.when(j == 0)
    def _():
        for lb in range(N_LANE_BLOCKS):
            u_ref[lb] = p[:, lb * LANES:(lb + 1) * LANES]

    @pl.when(j == 1)
    def _():
        q_ref[...] = (p * (HEAD_DIM ** -0.5)).astype(BF16)

    @pl.when(j == 2)
    def _():
        k_ref[...] = p
        kb_ref[...] = p.astype(BF16)

    @pl.when(j == 3)
    def _():
        v_ref[...] = p
        vb_ref[...] = p.astype(BF16)


def _mix_in(x, g, w_main, w_f, b_f):
    m = x.shape[0]
    tm = min(ROW_TILE, m)
    row = lambda i, j: (i, 0)
    out_shape = (
        jax.ShapeDtypeStruct((N_LANE_BLOCKS, m, LANES), F32),
        jax.ShapeDtypeStruct((m, D_ATTN), BF16),
        jax.ShapeDtypeStruct((m, D_ATTN), F32),
        jax.ShapeDtypeStruct((m, D_ATTN), F32),
        jax.ShapeDtypeStruct((m, D_ATTN), BF16),
        jax.ShapeDtypeStruct((m, D_ATTN), BF16),
        jax.ShapeDtypeStruct((m, LANES), F32),
    )
    out_specs = (
        pl.BlockSpec((N_LANE_BLOCKS, tm, LANES), lambda i, j: (0, i, 0)),
        pl.BlockSpec((tm, D_ATTN), row),
        pl.BlockSpec((tm, D_ATTN), row),
        pl.BlockSpec((tm, D_ATTN), row),
        pl.BlockSpec((tm, D_ATTN), row),
        pl.BlockSpec((tm, D_ATTN), row),
        pl.BlockSpec((tm, LANES), row),
    )
    return pl.pallas_call(
        _mix_in_kernel,
        grid=(m // tm, 4),
        in_specs=[
            pl.BlockSpec((tm, D_MODEL), row),
            pl.BlockSpec((1, D_MODEL), lambda i, j: (0, 0)),
            pl.BlockSpec((D_MODEL, D_ATTN), lambda i, j: (0, j)),
            pl.BlockSpec((D_MODEL, LANES), lambda i, j: (0, 0)),
            pl.BlockSpec((1, LANES), lambda i, j: (0, 0)),
        ],
        out_specs=out_specs,
        out_shape=out_shape,
        scratch_shapes=[pltpu.VMEM((tm, D_MODEL), BF16)],
        compiler_params=_cparams(("parallel", "arbitrary")),
        name="mix_in",
    )(x, g, w_main, w_f, b_f)


def _cumsum_kernel(x_ref, o_ref):
    blk = 512
    n = x_ref.shape[2]
    r = lax.broadcasted_iota(jnp.int32, (blk, blk), 0)
    c = lax.broadcasted_iota(jnp.int32, (blk, blk), 1)
    tri = (r <= c).astype(F32)
    carry = jnp.zeros((N_HEADS, 1), F32)
    for b in range(n // blk):
        xb = x_ref[0, :, b * blk:(b + 1) * blk]
        cb = jnp.dot(xb, tri, precision=lax.Precision.HIGHEST,
                     preferred_element_type=F32) + carry
        o_ref[0, :, b * blk:(b + 1) * blk] = cb
        carry = cb[:, blk - 1:blk]


def _cumsum_heads(lf_t):
    b, h, n = lf_t.shape
    return pl.pallas_call(
        _cumsum_kernel,
        grid=(b,),
        in_specs=[pl.BlockSpec((1, h, n), lambda i: (i, 0, 0))],
        out_specs=pl.BlockSpec((1, h, n), lambda i: (i, 0, 0)),
        out_shape=jax.ShapeDtypeStruct((b, h, n), F32),
        compiler_params=_cparams(("parallel",)),
        name="logf_cumsum",
    )(lf_t)


def _flash_kernel(q_ref, k_ref, v_ref, ccol_ref, crow_ref, o_ref, m_ref, l_ref, acc_ref):
    h = pl.program_id(1)
    qi = pl.program_id(2)
    t = ATT_TILE
    q = q_ref[0]
    cc = ccol_ref[0]
    lane = lax.broadcasted_iota(jnp.int32, cc.shape, 1)
    cq = jnp.sum(jnp.where(lane == h, cc, 0.0), axis=1, keepdims=True)

    m_ref[...] = jnp.full_like(m_ref, NEG_BIG)
    l_ref[...] = jnp.zeros_like(l_ref)
    acc_ref[...] = jnp.zeros_like(acc_ref)

    def step(ki, masked):
        start = pl.multiple_of(ki * t, t)
        k = k_ref[0, pl.ds(start, t), :]
        v = v_ref[0, pl.ds(start, t), :]
        ck = crow_ref[0, pl.ds(h, 1), pl.ds(start, t)]
        s = lax.dot_general(q, k, (((1,), (1,)), ((), ())), preferred_element_type=F32)
        s = s + (cq - ck)
        if masked:
            r = lax.broadcasted_iota(jnp.int32, (t, t), 0)
            c = lax.broadcasted_iota(jnp.int32, (t, t), 1)
            s = jnp.where(r >= c, s, NEG_BIG)
        m_old = m_ref[...]
        m_new = jnp.maximum(m_old, jnp.max(s, axis=1, keepdims=True))
        alpha = jnp.exp(m_old - m_new)
        p = jnp.exp(s - m_new)
        l_ref[...] = alpha * l_ref[...] + jnp.sum(p, axis=1, keepdims=True)
        acc_ref[...] = alpha * acc_ref[...] + jnp.dot(p.astype(BF16), v,
                                                      preferred_element_type=F32)
        m_ref[...] = m_new

    def body(ki, carry):
        step(ki, False)
        return carry

    lax.fori_loop(0, qi, body, 0)
    step(qi, True)
    o_ref[0] = acc_ref[...] / l_ref[...]


def _flash_prompt(q, kb, vb, c_col, c_row):
    b, n, _ = q.shape
    t = ATT_TILE
    return pl.pallas_call(
        _flash_kernel,
        grid=(b, N_HEADS, n // t),
        in_specs=[
            pl.BlockSpec((1, t, HEAD_DIM), lambda bi, h, qi: (bi, qi, h)),
            pl.BlockSpec((1, n, HEAD_DIM), lambda bi, h, qi: (bi, 0, h)),
            pl.BlockSpec((1, n, HEAD_DIM), lambda bi, h, qi: (bi, 0, h)),
            pl.BlockSpec((1, t, N_HEADS), lambda bi, h, qi: (bi, qi, 0)),
            pl.BlockSpec((1, N_HEADS, n), lambda bi, h, qi: (bi, 0, 0)),
        ],
        out_specs=pl.BlockSpec((1, t, HEAD_DIM), lambda bi, h, qi: (bi, qi, h)),
        out_shape=jax.ShapeDtypeStruct((b, n, D_ATTN), F32),
        scratch_shapes=[pltpu.VMEM((t, 1), F32), pltpu.VMEM((t, 1), F32),
                        pltpu.VMEM((t, HEAD_DIM), F32)],
        compiler_params=_cparams(("parallel", "parallel", "arbitrary")),
        name="flash_prompt",
    )(q, kb, vb, c_col, c_row)


def _sample_attn_kernel(pt_ref, q_ref, kn_ref, vn_ref, lfn_ref, *rest):
    g = PAGES_PER_STEP
    k_refs = rest[0:g]
    v_refs = rest[g:2 * g]
    lf_refs = rest[2 * g:3 * g]
    o_ref = rest[3 * g]
    m_ref, l_ref, acc_ref, carry_ref, qbd_ref = rest[3 * g + 1:]
    s_idx = pl.program_id(1)
    n_q = q_ref.shape[1]
    rows = n_q * N_HEADS
    page = k_refs[0].shape[2]

    @pl.when(s_idx == 0)
    def _():
        head_of_lane = lax.broadcasted_iota(jnp.int32, (N_HEADS, D_ATTN), 1) // HEAD_DIM
        head_of_row = lax.broadcasted_iota(jnp.int32, (N_HEADS, D_ATTN), 0)
        sel = head_of_lane == head_of_row
        for qq in range(n_q):
            row = jnp.broadcast_to(q_ref[0, qq:qq + 1, :], (N_HEADS, D_ATTN))
            qbd_ref[qq * N_HEADS:(qq + 1) * N_HEADS, :] = jnp.where(sel, row, jnp.zeros_like(row))
        m_ref[...] = jnp.full_like(m_ref, NEG_BIG)
        l_ref[...] = jnp.zeros_like(l_ref)
        acc_ref[...] = jnp.zeros_like(acc_ref)
        carry_ref[...] = jnp.zeros_like(carry_ref)

    lfn = lfn_ref[0]
    cn = [lfn[:, 0:1]]
    for qq in range(1, n_q):
        cn.append(cn[-1] + lfn[:, qq:qq + 1])

    qbd = qbd_ref[...].astype(BF16)
    r = lax.broadcasted_iota(jnp.int32, (page, page), 0)
    c = lax.broadcasted_iota(jnp.int32, (page, page), 1)
    later = (r > c).astype(F32)

    def online_update(s, v_bf):
        m_old = m_ref[...]
        m_new = jnp.maximum(m_old, jnp.max(s, axis=1, keepdims=True))
        alpha = jnp.exp(m_old - m_new)
        p = jnp.exp(s - m_new)
        l_ref[...] = alpha * l_ref[...] + jnp.sum(p, axis=1, keepdims=True)
        acc_ref[...] = alpha * acc_ref[...] + jnp.dot(p.astype(BF16), v_bf,
                                                      preferred_element_type=F32)
        m_ref[...] = m_new

    for i in range(g):
        kb = k_refs[i][0, 0].astype(BF16)
        vb = v_refs[i][0, 0].astype(BF16)
        lft = lf_refs[i][0, 0]
        suffix = jnp.dot(lft, later, precision=lax.Precision.HIGHEST,
                         preferred_element_type=F32) + carry_ref[...]
        carry_ref[...] = carry_ref[...] + jnp.sum(lft, axis=1, keepdims=True)
        s = lax.dot_general(qbd, kb, (((1,), (1,)), ((), ())), preferred_element_type=F32)
        bias = jnp.concatenate([suffix + cn[qq] for qq in range(n_q)], axis=0)
        online_update(s + bias, vb)

    @pl.when(s_idx == pl.num_programs(1) - 1)
    def _():
        kn = kn_ref[0].astype(BF16)
        vn = vn_ref[0].astype(BF16)
        s = lax.dot_general(qbd, kn, (((1,), (1,)), ((), ())), preferred_element_type=F32)
        key_pos = lax.broadcasted_iota(jnp.int32, lfn.shape, 1)
        lfn_c = jnp.zeros_like(lfn)
        for i in range(n_q):
            lfn_c = lfn_c + jnp.where(key_pos >= i, lfn[:, i:i + 1], 0.0)
        bias = jnp.concatenate(
            [jnp.where(key_pos <= qq, cn[qq] - lfn_c, NEG_BIG) for qq in range(n_q)], axis=0)
        s = jnp.where(bias > 0.5 * NEG_BIG, s + bias, NEG_BIG)
        online_update(s, vn)
        out = acc_ref[...] / l_ref[...]
        head_of_lane = lax.broadcasted_iota(jnp.int32, (N_HEADS, D_ATTN), 1) // HEAD_DIM
        head_of_row = lax.broadcasted_iota(jnp.int32, (N_HEADS, D_ATTN), 0)
        sel = head_of_lane == head_of_row
        for qq in range(n_q):
            blk = out[qq * N_HEADS:(qq + 1) * N_HEADS, :]
            o_ref[0, qq:qq + 1, :] = jnp.sum(jnp.where(sel, blk, 0.0), axis=0, keepdims=True)


def _sample_attn(q, k_new, v_new, lf_new_t, cache_k, cache_v, cache_lf_t, page_table, layer):
    n, n_q, _ = q.shape
    n_pages = page_table.shape[1]
    page = cache_k.shape[2]
    g = PAGES_PER_STEP
    steps = n_pages // g
    rows = n_q * N_HEADS
    n_key = NEW_KEY_PAD
    k_new = jnp.pad(k_new, ((0, 0), (0, n_key - n_q), (0, 0)))
    v_new = jnp.pad(v_new, ((0, 0), (0, n_key - n_q), (0, 0)))
    lf_new_t = jnp.pad(lf_new_t, ((0, 0), (0, 0), (0, n_key - n_q)))

    def page_map(i):
        return lambda b, s, pt: (layer, pt[b, n_pages - 1 - (s * g + i)], 0, 0)

    per_seq = lambda b, s, pt: (b, 0, 0)
    in_specs = [
        pl.BlockSpec((1, n_q, D_ATTN), per_seq),
        pl.BlockSpec((1, n_key, D_ATTN), per_seq),
        pl.BlockSpec((1, n_key, D_ATTN), per_seq),
        pl.BlockSpec((1, N_HEADS, n_key), per_seq),
    ]
    in_specs += [pl.BlockSpec((1, 1, page, D_ATTN), page_map(i)) for i in range(g)]
    in_specs += [pl.BlockSpec((1, 1, page, D_ATTN), page_map(i)) for i in range(g)]
    in_specs += [pl.BlockSpec((1, 1, N_HEADS, page), page_map(i)) for i in range(g)]
    grid_spec = pltpu.PrefetchScalarGridSpec(
        num_scalar_prefetch=1,
        grid=(n, steps),
        in_specs=in_specs,
        out_specs=pl.BlockSpec((1, n_q, D_ATTN), per_seq),
        scratch_shapes=[pltpu.VMEM((rows, 1), F32), pltpu.VMEM((rows, 1), F32),
                        pltpu.VMEM((rows, D_ATTN), F32), pltpu.VMEM((N_HEADS, 1), F32),
                        pltpu.VMEM((rows, D_ATTN), F32)],
    )
    return pl.pallas_call(
        _sample_attn_kernel,
        grid_spec=grid_spec,
        out_shape=jax.ShapeDtypeStruct((n, n_q, D_ATTN), F32),
        compiler_params=_cparams(("parallel", "arbitrary")),
        name="sample_attn",
    )(page_table, q, k_new, v_new, lf_new_t,
      *([cache_k] * g), *([cache_v] * g), *([cache_lf_t] * g))


def _discretise(a_re, a_im, log_dt):
    dt = jnp.exp(log_dt)
    mag = jnp.exp(dt * a_re)
    ab_re = mag * jnp.cos(dt * a_im)
    ab_im = mag * jnp.sin(dt * a_im)
    den = a_re * a_re + a_im * a_im
    f_re = ((ab_re - 1.0) * a_re + ab_im * a_im) / den
    f_im = (ab_im * a_re - (ab_re - 1.0) * a_im) / den
    return ab_re, ab_im, f_re, f_im


def _s5_prep_kernel(pe_ref, b_ref, pf_ref, c_ref, ps_ref,
                    e_ref, f_ref, m_ref, pw_ref, e32_ref, f0_ref):
    t_len = S5_CHUNK
    sc = STATE_COLS
    ab_re, ab_im, f_re, f_im = _discretise(pe_ref[0, 0], pe_ref[0, 1], pe_ref[0, 2])
    b_re = b_ref[0, 0]
    b_im = b_ref[0, 1]
    row_g = lax.broadcasted_iota(jnp.int32, (LANES, sc), 0) // SSM_GROUP
    col_g = lax.broadcasted_iota(jnp.int32, (LANES, sc), 1) // SSM_STATE
    same = row_g == col_g
    w_re = jnp.where(same, f_re * b_re - f_im * b_im, 0.0)
    w_im = jnp.where(same, f_re * b_im + f_im * b_re, 0.0)
    for j in range(t_len):
        s = t_len - 1 - j
        e32_ref[s * LANES:(s + 1) * LANES, 0:sc] = w_re
        e32_ref[s * LANES:(s + 1) * LANES, sc:2 * sc] = w_im
        w_re, w_im = ab_re * w_re - ab_im * w_im, ab_re * w_im + ab_im * w_re
    e_ref[0] = e32_ref[...].astype(BF16)

    ab_re, ab_im, _, _ = _discretise(pf_ref[0, 0], pf_ref[0, 1], pf_ref[0, 2])
    row_g = lax.broadcasted_iota(jnp.int32, (sc, LANES), 0) // SSM_STATE
    col_g = lax.broadcasted_iota(jnp.int32, (sc, LANES), 1) // SSM_GROUP
    same = row_g == col_g
    g_re = jnp.where(same, c_ref[0, 0], 0.0)
    g_im = jnp.where(same, c_ref[0, 1], 0.0)
    f0_ref[0:sc, :] = g_re
    f0_ref[sc:2 * sc, :] = -g_im
    for t in range(t_len):
        g_re, g_im = g_re * ab_re - g_im * ab_im, g_re * ab_im + g_im * ab_re
        f_ref[0, 0:sc, t * LANES:(t + 1) * LANES] = g_re.astype(BF16)
        f_ref[0, sc:2 * sc, t * LANES:(t + 1) * LANES] = (-g_im).astype(BF16)

    lag = jnp.dot(e32_ref[...], f0_ref[...], precision=lax.Precision.HIGHEST,
                  preferred_element_type=F32).astype(BF16)
    m_ref[0] = jnp.zeros(m_ref.shape[1:], BF16)
    for t in range(t_len):
        m_ref[0, 0:(t + 1) * LANES, t * LANES:(t + 1) * LANES] = lag[(t_len - 1 - t) * LANES:, :]

    a_re, a_im, _, _ = _discretise(ps_ref[0, 0:1], ps_ref[0, 1:2], ps_ref[0, 2:3])
    a2_re, a2_im = a_re * a_re - a_im * a_im, 2.0 * a_re * a_im
    a4_re, a4_im = a2_re * a2_re - a2_im * a2_im, 2.0 * a2_re * a2_im
    a8_re, a8_im = a4_re * a4_re - a4_im * a4_im, 2.0 * a4_re * a4_im
    a16_re, a16_im = a8_re * a8_re - a8_im * a8_im, 2.0 * a8_re * a8_im
    pw_ref[0, 0:1, 0:sc] = a4_re
    pw_ref[0, 0:1, sc:2 * sc] = a4_im
    pw_ref[0, 1:2, 0:sc] = a16_re
    pw_ref[0, 1:2, sc:2 * sc] = a16_im


def _s5_prep(a_re, a_im, log_dt, b_re, b_im, c_re, c_im):
    nb, gb, st, ch = N_LANE_BLOCKS, GROUPS_PER_BLOCK, SSM_STATE, SSM_GROUP
    ldt = jnp.broadcast_to(log_dt[:, None], (SSM_GROUPS, st))
    par = jnp.stack([a_re, a_im, ldt], axis=0).reshape(3, nb, gb, st)
    pe = jnp.broadcast_to(par[:, :, :, None, None, :], (3, nb, gb, ch, gb, st))
    pe = pe.reshape(3, nb, LANES, STATE_COLS).transpose(1, 0, 2, 3)
    bb = jnp.stack([b_re, b_im], axis=0).reshape(2, nb, gb, st, ch).transpose(0, 1, 2, 4, 3)
    bb = jnp.broadcast_to(bb[:, :, :, :, None, :], (2, nb, gb, ch, gb, st))
    bb = bb.reshape(2, nb, LANES, STATE_COLS).transpose(1, 0, 2, 3)
    pf = jnp.broadcast_to(par[:, :, :, :, None, None], (3, nb, gb, st, gb, ch))
    pf = pf.reshape(3, nb, STATE_COLS, LANES).transpose(1, 0, 2, 3)
    cc = jnp.stack([c_re, c_im], axis=0).reshape(2, nb, gb, ch, st).transpose(0, 1, 2, 4, 3)
    cc = jnp.broadcast_to(cc[:, :, :, :, None, :], (2, nb, gb, st, gb, ch))
    cc = cc.reshape(2, nb, STATE_COLS, LANES).transpose(1, 0, 2, 3)
    ps = par.reshape(3, nb, STATE_COLS).transpose(1, 0, 2)

    tl = S5_CHUNK * LANES
    blk4 = lambda i: (i, 0, 0, 0)
    blk3 = lambda i: (i, 0, 0)
    return pl.pallas_call(
        _s5_prep_kernel,
        grid=(nb,),
        in_specs=[
            pl.BlockSpec((1, 3, LANES, STATE_COLS), blk4),
            pl.BlockSpec((1, 2, LANES, STATE_COLS), blk4),
            pl.BlockSpec((1, 3, STATE_COLS, LANES), blk4),
            pl.BlockSpec((1, 2, STATE_COLS, LANES), blk4),
            pl.BlockSpec((1, 3, STATE_COLS), blk3),
        ],
        out_specs=(
            pl.BlockSpec((1, tl, 2 * STATE_COLS), blk3),
            pl.BlockSpec((1, 2 * STATE_COLS, tl), blk3),
            pl.BlockSpec((1, tl, tl), blk3),
            pl.BlockSpec((1, 2, 2 * STATE_COLS), blk3),
        ),
        out_shape=(
            jax.ShapeDtypeStruct((nb, tl, 2 * STATE_COLS), BF16),
            jax.ShapeDtypeStruct((nb, 2 * STATE_COLS, tl), BF16),
            jax.ShapeDtypeStruct((nb, tl, tl), BF16),
            jax.ShapeDtypeStruct((nb, 2, 2 * STATE_COLS), F32),
        ),
        scratch_shapes=[pltpu.VMEM((tl, 2 * STATE_COLS), F32),
                        pltpu.VMEM((2 * STATE_COLS, LANES), F32)],
        compiler_params=_cparams(("parallel",)),
        name="s5_prep",
    )(pe, bb, pf, cc, ps)


def _s5_prompt_kernel(u_ref, e_ref, m_ref, f_ref, pw_ref, d_ref,
                      y_ref, hl_ref, ub_ref, sloc_ref, hp_ref, hpb_ref):
    ct = pl.program_id(1)
    n_rows = u_ref.shape[1]
    n_batch = hl_ref.shape[1]
    per_batch = n_rows // n_batch
    sc = STATE_COLS

    @pl.when(ct == 0)
    def _():
        ub = u_ref[0].astype(BF16)
        ub_ref[...] = ub
        sloc_ref[...] = jnp.dot(ub, e_ref[0], preferred_element_type=F32)
        p_re = pw_ref[0, 1:2, 0:sc]
        p_im = pw_ref[0, 1:2, sc:2 * sc]

        def body(k, carry):
            new = []
            for b in range(n_batch):
                h_re, h_im = carry[2 * b], carry[2 * b + 1]
                row = b * per_batch + k
                hp_ref[pl.ds(row, 1), 0:sc] = h_re
                hp_ref[pl.ds(row, 1), sc:2 * sc] = h_im
                s_re = sloc_ref[pl.ds(row, 1), 0:sc]
                s_im = sloc_ref[pl.ds(row, 1), sc:2 * sc]
                new.append(p_re * h_re - p_im * h_im + s_re)
                new.append(p_re * h_im + p_im * h_re + s_im)
            return tuple(new)

        zero = jnp.zeros((1, sc), F32)
        final = lax.fori_loop(0, per_batch, body, (zero,) * (2 * n_batch))
        for b in range(n_batch):
            hl_ref[0, b:b + 1, 0:sc] = final[2 * b]
            hl_ref[0, b:b + 1, sc:2 * sc] = final[2 * b + 1]
        hpb_ref[...] = hp_ref[...].astype(BF16)

    w = y_ref.shape[2]
    col = pl.multiple_of(ct * w, w)
    y = jnp.dot(ub_ref[...], m_ref[0], preferred_element_type=F32)
    y = y + jnp.dot(hpb_ref[...], f_ref[0], preferred_element_type=F32)
    y_ref[0] = y + d_ref[0] * u_ref[0, :, pl.ds(col, w)]


def _s5_prompt(u_chunks, e_op, m_op, f_op, pw, d_tiled, n_batch):
    nb, n_rows, tl = u_chunks.shape
    w = 512
    return pl.pallas_call(
        _s5_prompt_kernel,
        grid=(nb, tl // w),
        in_specs=[
            pl.BlockSpec((1, n_rows, tl), lambda i, c: (i, 0, 0)),
            pl.BlockSpec((1, tl, 2 * STATE_COLS), lambda i, c: (i, 0, 0)),
            pl.BlockSpec((1, tl, w), lambda i, c: (i, 0, c)),
            pl.BlockSpec((1, 2 * STATE_COLS, w), lambda i, c: (i, 0, c)),
            pl.BlockSpec((1, 2, 2 * STATE_COLS), lambda i, c: (i, 0, 0)),
            pl.BlockSpec((1, 1, w), lambda i, c: (i, 0, 0)),
        ],
        out_specs=(
            pl.BlockSpec((1, n_rows, w), lambda i, c: (i, 0, c)),
            pl.BlockSpec((1, n_batch, 2 * STATE_COLS), lambda i, c: (i, 0, 0)),
        ),
        out_shape=(
            jax.ShapeDtypeStruct((nb, n_rows, tl), F32),
            jax.ShapeDtypeStruct((nb, n_batch, 2 * STATE_COLS), F32),
        ),
        scratch_shapes=[pltpu.VMEM((n_rows, tl), BF16),
                        pltpu.VMEM((n_rows, 2 * STATE_COLS), F32),
                        pltpu.VMEM((n_rows, 2 * STATE_COLS), F32),
                        pltpu.VMEM((n_rows, 2 * STATE_COLS), BF16)],
        compiler_params=_cparams(("parallel", "arbitrary")),
        name="s5_prompt",
    )(u_chunks, e_op, m_op, f_op, pw, d_tiled)


def _s5_sample_kernel(u_ref, e_ref, m_ref, f_ref, pw_ref, d_ref, h0_ref, y_ref, hl_ref):
    sc = STATE_COLS
    u = u_ref[0]
    ub = u.astype(BF16)
    h0 = h0_ref[0]
    y = jnp.dot(ub, m_ref[0], preferred_element_type=F32)
    y = y + jnp.dot(h0.astype(BF16), f_ref[0], preferred_element_type=F32)
    y_ref[0] = y + d_ref[0] * u
    sloc = jnp.dot(ub, e_ref[0], preferred_element_type=F32)
    p_re = pw_ref[0, 0:1, 0:sc]
    p_im = pw_ref[0, 0:1, sc:2 * sc]
    h_re = h0[:, 0:sc]
    h_im = h0[:, sc:2 * sc]
    hl_ref[0, :, 0:sc] = p_re * h_re - p_im * h_im + sloc[:, 0:sc]
    hl_ref[0, :, sc:2 * sc] = p_re * h_im + p_im * h_re + sloc[:, sc:2 * sc]


def _s5_sample(u_chunks, e_op, m_op, f_op, pw, d_tiled, h0):
    nb, n_seq, w = u_chunks.shape
    tl = e_op.shape[1]
    e_blk = (tl - w) // w
    return pl.pallas_call(
        _s5_sample_kernel,
        grid=(nb,),
        in_specs=[
            pl.BlockSpec((1, n_seq, w), lambda i: (i, 0, 0)),
            pl.BlockSpec((1, w, 2 * STATE_COLS), lambda i: (i, e_blk, 0)),
            pl.BlockSpec((1, w, w), lambda i: (i, 0, 0)),
            pl.BlockSpec((1, 2 * STATE_COLS, w), lambda i: (i, 0, 0)),
            pl.BlockSpec((1, 2, 2 * STATE_COLS), lambda i: (i, 0, 0)),
            pl.BlockSpec((1, 1, w), lambda i: (i, 0, 0)),
            pl.BlockSpec((1, n_seq, 2 * STATE_COLS), lambda i: (i, 0, 0)),
        ],
        out_specs=(
            pl.BlockSpec((1, n_seq, w), lambda i: (i, 0, 0)),
            pl.BlockSpec((1, n_seq, 2 * STATE_COLS), lambda i: (i, 0, 0)),
        ),
        out_shape=(
            jax.ShapeDtypeStruct((nb, n_seq, w), F32),
            jax.ShapeDtypeStruct((nb, n_seq, 2 * STATE_COLS), F32),
        ),
        compiler_params=_cparams(("parallel",)),
        name="s5_sample",
    )(u_chunks, e_op, m_op, f_op, pw, d_tiled, h0)


def _mix_out_kernel(x_ref, attn_ref, y_ref, gw_ref, gb_ref, ga_ref, gs_ref, wo_ref, gp_ref,
                    o_ref, ycat_ref, mg_ref):
    for lb in range(N_LANE_BLOCKS):
        ycat_ref[:, lb * LANES:(lb + 1) * LANES] = jax.nn.gelu(y_ref[lb])
    y = ycat_ref[...]
    z = jnp.dot(y.astype(BF16), gw_ref[...], preferred_element_type=F32) + gb_ref[...]
    s = y * jax.nn.sigmoid(z)
    mg_ref[:, 0:D_ATTN] = _rms(attn_ref[...], ga_ref[...]).astype(BF16)
    mg_ref[:, D_ATTN:] = _rms(s, gs_ref[...]).astype(BF16)
    m = jnp.dot(mg_ref[...], wo_ref[...], preferred_element_type=F32)
    o_ref[...] = x_ref[...] + _rms(m, gp_ref[...])


def _mix_out(x, attn, y_blk, glu_w, glu_b, g_attn, g_ssm, w_out, g_post):
    m = x.shape[0]
    tm = 256
    row = lambda i: (i, 0)
    const = lambda i: (0, 0)
    return pl.pallas_call(
        _mix_out_kernel,
        grid=(m // tm,),
        in_specs=[
            pl.BlockSpec((tm, D_MODEL), row),
            pl.BlockSpec((tm, D_ATTN), row),
            pl.BlockSpec((N_LANE_BLOCKS, tm, LANES), lambda i: (0, i, 0)),
            pl.BlockSpec((D_SSM, D_SSM), const),
            pl.BlockSpec((1, D_SSM), const),
            pl.BlockSpec((1, D_ATTN), const),
            pl.BlockSpec((1, D_SSM), const),
            pl.BlockSpec((D_MODEL, D_MODEL), const),
            pl.BlockSpec((1, D_MODEL), const),
        ],
        out_specs=pl.BlockSpec((tm, D_MODEL), row),
        out_shape=jax.ShapeDtypeStruct((m, D_MODEL), F32),
        scratch_shapes=[pltpu.VMEM((tm, D_SSM), F32), pltpu.VMEM((tm, D_MODEL), BF16)],
        compiler_params=_cparams(("parallel",)),
        name="mix_out",
    )(x, attn, y_blk, glu_w, glu_b, g_attn, g_ssm, w_out, g_post)


def _state_to_blocks(h):
    n = h.shape[0]
    h = h.reshape(n, N_LANE_BLOCKS, STATE_COLS, 2)
    return h.transpose(1, 0, 3, 2).reshape(N_LANE_BLOCKS, n, 2 * STATE_COLS)


def _blocks_to_state(h):
    n = h.shape[1]
    h = h.reshape(N_LANE_BLOCKS, n, 2, STATE_COLS).transpose(1, 0, 3, 2)
    return h.reshape(n, SSM_GROUPS, SSM_STATE, 2)


def _layer_weights(l, norm_g, ffn_gate, ffn_up, ffn_down, w_in, b_forget, glu_w, glu_b,
                   g_attn_out, g_ssm_out, w_out, ssm_d):
    pad = D_FF_PAD - D_FF
    w = {}
    w["ng"] = [norm_g[l, i].reshape(1, D_MODEL) for i in range(6)]
    w["ffn"] = []
    for i in range(2):
        wg = jnp.pad(ffn_gate[l, i].astype(BF16), ((0, 0), (0, pad)))
        wu = jnp.pad(ffn_up[l, i].astype(BF16), ((0, 0), (0, pad)))
        wd = jnp.pad(ffn_down[l, i].astype(BF16), ((0, pad), (0, 0)))
        w["ffn"].append((wg, wu, wd))
    n_main = D_SSM + 3 * D_ATTN
    w["w_main"] = w_in[l, :, :n_main].astype(BF16)
    w["w_f"] = jnp.pad(w_in[l, :, n_main:].astype(BF16), ((0, 0), (0, LANES - N_HEADS)))
    w["b_f"] = jnp.pad(b_forget[l].reshape(1, N_HEADS), ((0, 0), (0, LANES - N_HEADS)))
    w["glu_w"] = glu_w[l].astype(BF16)
    w["glu_b"] = glu_b[l].reshape(1, D_SSM)
    w["g_attn"] = g_attn_out[l].reshape(1, D_ATTN)
    w["g_ssm"] = g_ssm_out[l].reshape(1, D_SSM)
    w["w_out"] = w_out[l].astype(BF16)
    w["d_blk"] = ssm_d[l].reshape(N_LANE_BLOCKS, 1, LANES)
    return w


def _trunk_layer(x, w, mixer):
    ng = w["ng"]
    x = _ffn_half(x, ng[0], ng[1], *w["ffn"][0])
    u_blk, q, k, v, kb, vb, lf = _mix_in(x, ng[2], w["w_main"], w["w_f"], w["b_f"])
    logf = lf[:, :N_HEADS]
    attn, y_blk, h_last = mixer(u_blk, q, k, v, kb, vb, logf)
    x = _mix_out(x, attn, y_blk, w["glu_w"], w["glu_b"], w["g_attn"], w["g_ssm"],
                 w["w_out"], ng[3])
    x = _ffn_half(x, ng[4], ng[5], *w["ffn"][1])
    return x, k, v, logf, h_last


def kernel(x_prompt, x_sample, cache_k, cache_v, cache_logf, state_ssm, page_table, norm_g, ffn_gate, ffn_up, ffn_down, w_in, b_forget, ssm_a_re, ssm_a_im, ssm_log_dt, ssm_b_re, ssm_b_im, ssm_c_re, ssm_c_im, ssm_d, glu_w, glu_b, g_attn_out, g_ssm_out, w_out):
    batch, seq, _ = x_prompt.shape
    n_seq, n_new, _ = x_sample.shape
    depth = norm_g.shape[0]
    n_pool, page = cache_k.shape[1], cache_k.shape[2]
    assert seq % ROW_TILE == 0 and seq % ATT_TILE == 0 and seq % S5_CHUNK == 0
    assert (n_seq * n_new) % 256 == 0 and page_table.shape[1] % PAGES_PER_STEP == 0
    assert n_new * N_HEADS <= LANES

    xp = x_prompt.reshape(batch * seq, D_MODEL)
    xs = x_sample.reshape(n_seq * n_new, D_MODEL)
    ck = cache_k.reshape(depth, n_pool, page, D_ATTN)
    cv = cache_v.reshape(depth, n_pool, page, D_ATTN)
    clf_t = cache_logf.transpose(0, 1, 3, 2)

    outs = {name: [] for name in ("kp", "vp", "lfp", "sp", "ks", "vs", "lfs", "ss")}
    for l in range(depth):
        w = _layer_weights(l, norm_g, ffn_gate, ffn_up, ffn_down, w_in, b_forget, glu_w, glu_b,
                           g_attn_out, g_ssm_out, w_out, ssm_d)
        e_op, f_op, m_op, pw = _s5_prep(ssm_a_re[l], ssm_a_im[l], ssm_log_dt[l],
                                        ssm_b_re[l], ssm_b_im[l], ssm_c_re[l], ssm_c_im[l])
        d_prompt = jnp.tile(w["d_blk"], (1, 1, 512 // LANES))
        d_sample = jnp.tile(w["d_blk"], (1, 1, n_new))

        def prompt_mixer(u_blk, q, k, v, kb, vb, logf):
            lf_t = logf.reshape(batch, seq, N_HEADS).transpose(0, 2, 1)
            c_row = _cumsum_heads(lf_t)
            c_col = c_row.transpose(0, 2, 1)
            attn = _flash_prompt(q.reshape(batch, seq, D_ATTN), kb.reshape(batch, seq, D_ATTN),
                                 vb.reshape(batch, seq, D_ATTN), c_col, c_row)
            u_chunks = u_blk.reshape(N_LANE_BLOCKS, batch * seq // S5_CHUNK, S5_CHUNK * LANES)
            y, h_last = _s5_prompt(u_chunks, e_op, m_op, f_op, pw, d_prompt, batch)
            return (attn.reshape(batch * seq, D_ATTN),
                    y.reshape(N_LANE_BLOCKS, batch * seq, LANES), _blocks_to_state(h_last))

        def sample_mixer(u_blk, q, k, v, kb, vb, logf):
            lf_new_t = logf.reshape(n_seq, n_new, N_HEADS).transpose(0, 2, 1)
            attn = _sample_attn(q.astype(F32).reshape(n_seq, n_new, D_ATTN),
                                k.reshape(n_seq, n_new, D_ATTN),
                                v.reshape(n_seq, n_new, D_ATTN), lf_new_t, ck, cv, clf_t,
                                page_table, l)
            u_chunks = u_blk.reshape(N_LANE_BLOCKS, n_seq, n_new * LANES)
            y, h_last = _s5_sample(u_chunks, e_op, m_op, f_op, pw, d_sample,
                                   _state_to_blocks(state_ssm[l]))
            return (attn.reshape(n_seq * n_new, D_ATTN),
                    y.reshape(N_LANE_BLOCKS, n_seq * n_new, LANES), _blocks_to_state(h_last))

        xp, k1, v1, lf1, s1 = _trunk_layer(xp, w, prompt_mixer)
        xs, k2, v2, lf2, s2 = _trunk_layer(xs, w, sample_mixer)
        outs["kp"].append(k1.reshape(batch, seq, N_HEADS, HEAD_DIM))
        outs["vp"].append(v1.reshape(batch, seq, N_HEADS, HEAD_DIM))
        outs["lfp"].append(lf1.reshape(batch, seq, N_HEADS))
        outs["sp"].append(s1)
        outs["ks"].append(k2.reshape(n_seq, n_new, N_HEADS, HEAD_DIM))
        outs["vs"].append(v2.reshape(n_seq, n_new, N_HEADS, HEAD_DIM))
        outs["lfs"].append(lf2.reshape(n_seq, n_new, N_HEADS))
        outs["ss"].append(s2)

    return (xp.reshape(batch, seq, D_MODEL), xs.reshape(n_seq, n_new, D_MODEL),
            jnp.stack(outs["kp"]), jnp.stack(outs["vp"]), jnp.stack(outs["lfp"]),
            jnp.stack(outs["sp"]),
            jnp.stack(outs["ks"]), jnp.stack(outs["vs"]), jnp.stack(outs["lfs"]),
            jnp.stack(outs["ss"]))
```

```python
import math

import jax
import jax.numpy as jnp
from jax import lax
from jax.experimental import pallas as pl
from jax.experimental.pallas import tpu as pltpu

F32 = jnp.float32
BF16 = jnp.bfloat16

D_MODEL = 2048
D_SSM = 1024
D_ATTN = 1024
HEAD_DIM = 128
N_HEADS = 8
SSM_GROUP = 16
SSM_GROUPS = 64
SSM_STATE = 64
D_FF = 5504
EPS = 1e-6
LOG2E = math.log2(math.e)

LANES = 128
SUBLANES = 8
GROUPS_PER_BLOCK = LANES // SSM_GROUP
N_LANE_BLOCKS = D_SSM // LANES
STATE_COLS = GROUPS_PER_BLOCK * SSM_STATE
S5_CHUNK = 16
FF_TILE = 512
D_FF_PAD = ((D_FF + FF_TILE - 1) // FF_TILE) * FF_TILE
ROW_TILE = 512
ATT_TILE = 512
HEADS_PER_STEP = 2
PAGES_PER_STEP = 8
DECAY_PAGES = 256
VMEM_LIMIT = 56 * 1024 * 1024
NEG_BIG = -1e30


def _cparams(sem):
    return pltpu.CompilerParams(dimension_semantics=sem, vmem_limit_bytes=VMEM_LIMIT)


def _rms(x, g):
    ms = jnp.mean(x * x, axis=-1, keepdims=True)
    return x * lax.rsqrt(ms + EPS) * g


def _log_sigmoid(z):
    return jnp.minimum(z, 0.0) - jnp.log1p(jnp.exp(-jnp.abs(z)))


def _ffn_kernel(x_ref, gpre_ref, gpost_ref, wg_ref, wu_ref, wd_ref, o_ref, h_ref, acc_ref):
    j = pl.program_id(1)

    @pl.when(j == 0)
    def _():
        h_ref[...] = _rms(x_ref[...], gpre_ref[...]).astype(BF16)
        acc_ref[...] = jnp.zeros_like(acc_ref)

    h = h_ref[...]
    g = jnp.dot(h, wg_ref[...], preferred_element_type=F32)
    u = jnp.dot(h, wu_ref[...], preferred_element_type=F32)
    a = (g * jax.nn.sigmoid(g) * u).astype(BF16)
    acc_ref[...] += jnp.dot(a, wd_ref[...], preferred_element_type=F32)

    @pl.when(j == pl.num_programs(1) - 1)
    def _():
        o_ref[...] = x_ref[...] + 0.5 * _rms(acc_ref[...], gpost_ref[...])


def _ffn_half(x, g_pre, g_post, wg, wu, wd, layer, half):
    m = x.shape[0]
    tm = min(ROW_TILE, m)
    grid = (m // tm, D_FF_PAD // FF_TILE)
    return pl.pallas_call(
        _ffn_kernel,
        grid=grid,
        in_specs=[
            pl.BlockSpec((tm, D_MODEL), lambda i, j: (i, 0)),
            pl.BlockSpec((1, D_MODEL), lambda i, j: (0, 0)),
            pl.BlockSpec((1, D_MODEL), lambda i, j: (0, 0)),
            pl.BlockSpec((None, None, D_MODEL, FF_TILE), lambda i, j: (layer, half, 0, j)),
            pl.BlockSpec((None, None, D_MODEL, FF_TILE), lambda i, j: (layer, half, 0, j)),
            pl.BlockSpec((None, None, FF_TILE, D_MODEL), lambda i, j: (layer, half, j, 0)),
        ],
        out_specs=pl.BlockSpec((tm, D_MODEL), lambda i, j: (i, 0)),
        out_shape=jax.ShapeDtypeStruct((m, D_MODEL), F32),
        scratch_shapes=[pltpu.VMEM((tm, D_MODEL), BF16), pltpu.VMEM((tm, D_MODEL), F32)],
        compiler_params=_cparams(("parallel", "arbitrary")),
        name="ffn_half",
    )(x, g_pre, g_post, wg, wu, wd)


def _mix_in_kernel(x_ref, g_ref, w_ref, wf_ref, bf_ref,
                   u_ref, q_ref, k_ref, v_ref, kb_ref, vb_ref, lf_ref, h_ref):
    j = pl.program_id(1)

    @pl.when(j == 0)
    def _():
        h = _rms(x_ref[...], g_ref[...]).astype(BF16)
        h_ref[...] = h
        z = jnp.dot(h, wf_ref[...], preferred_element_type=F32) + bf_ref[...]
        lf_ref[...] = _log_sigmoid(z)

    p = jnp.dot(h_ref[...], w_ref[...], preferred_element_type=F32)

    @pl.when(j == 0)
    def _():
        for lb in range(N_LANE_BLOCKS):
            u_ref[lb] = p[:, lb * LANES:(lb + 1) * LANES]

    @pl.when(j == 1)
    def _():
        q_ref[...] = (p * (HEAD_DIM ** -0.5 * LOG2E)).astype(BF16)

    @pl.when(j == 2)
    def _():
        k_ref[...] = p
        kb_ref[...] = p.astype(BF16)

    @pl.when(j == 3)
    def _():
        v_ref[...] = p
        vb_ref[...] = p.astype(BF16)


def _mix_in(x, g, w_main, w_f, b_f, layer):
    m = x.shape[0]
    tm = min(ROW_TILE, m)
    row = lambda i, j: (i, 0)
    out_shape = (
        jax.ShapeDtypeStruct((N_LANE_BLOCKS, m, LANES), F32),
        jax.ShapeDtypeStruct((m, D_ATTN), BF16),
        jax.ShapeDtypeStruct((m, D_ATTN), F32),
        jax.ShapeDtypeStruct((m, D_ATTN), F32),
        jax.ShapeDtypeStruct((m, D_ATTN), BF16),
        jax.ShapeDtypeStruct((m, D_ATTN), BF16),
        jax.ShapeDtypeStruct((m, LANES), F32),
    )
    out_specs = (
        pl.BlockSpec((N_LANE_BLOCKS, tm, LANES), lambda i, j: (0, i, 0)),
        pl.BlockSpec((tm, D_ATTN), row),
        pl.BlockSpec((tm, D_ATTN), row),
        pl.BlockSpec((tm, D_ATTN), row),
        pl.BlockSpec((tm, D_ATTN), row),
        pl.BlockSpec((tm, D_ATTN), row),
        pl.BlockSpec((tm, LANES), row),
    )
    return pl.pallas_call(
        _mix_in_kernel,
        grid=(m // tm, 4),
        in_specs=[
            pl.BlockSpec((tm, D_MODEL), row),
            pl.BlockSpec((1, D_MODEL), lambda i, j: (0, 0)),
            pl.BlockSpec((None, D_MODEL, D_ATTN), lambda i, j: (layer, 0, j)),
            pl.BlockSpec((None, D_MODEL, LANES), lambda i, j: (layer, 0, 0)),
            pl.BlockSpec((None, 1, LANES), lambda i, j: (layer, 0, 0)),
        ],
        out_specs=out_specs,
        out_shape=out_shape,
        scratch_shapes=[pltpu.VMEM((tm, D_MODEL), BF16)],
        compiler_params=_cparams(("parallel", "arbitrary")),
        name="mix_in",
    )(x, g, w_main, w_f, b_f)


def _cumsum_kernel(x_ref, o_ref):
    blk = 512
    n = x_ref.shape[2]
    r = lax.broadcasted_iota(jnp.int32, (blk, blk), 0)
    c = lax.broadcasted_iota(jnp.int32, (blk, blk), 1)
    tri = (r <= c).astype(F32)
    carry = jnp.zeros((N_HEADS, 1), F32)
    for b in range(n // blk):
        xb = x_ref[0, :, b * blk:(b + 1) * blk]
        cb = jnp.dot(xb, tri, precision=lax.Precision.HIGHEST,
                     preferred_element_type=F32) + carry
        o_ref[0, :, b * blk:(b + 1) * blk] = cb * LOG2E
        carry = cb[:, blk - 1:blk]


def _cumsum_heads(lf_t):
    b, h, n = lf_t.shape
    return pl.pallas_call(
        _cumsum_kernel,
        grid=(b,),
        in_specs=[pl.BlockSpec((1, h, n), lambda i: (i, 0, 0))],
        out_specs=pl.BlockSpec((1, h, n), lambda i: (i, 0, 0)),
        out_shape=jax.ShapeDtypeStruct((b, h, n), F32),
        compiler_params=_cparams(("parallel",)),
        name="logf_cumsum",
    )(lf_t)


def _flash_kernel(q_ref, kt_ref, v_ref, crow_ref, o_ref, m_ref, l_ref, acc_ref):
    hb = pl.program_id(1)
    qi = pl.program_id(2)
    t = ATT_TILE

    m_ref[...] = jnp.full_like(m_ref, NEG_BIG)
    l_ref[...] = jnp.zeros_like(l_ref)
    acc_ref[...] = jnp.zeros_like(acc_ref)

    def step(ki, masked):
        start = pl.multiple_of(ki * t, t)
        for a in range(HEADS_PER_STEP):
            q = q_ref[0, :, a * HEAD_DIM:(a + 1) * HEAD_DIM]
            kt = kt_ref[0, a * HEAD_DIM:(a + 1) * HEAD_DIM, pl.ds(start, t)]
            v = v_ref[0, pl.ds(start, t), a * HEAD_DIM:(a + 1) * HEAD_DIM]
            ck = crow_ref[0, pl.ds(hb * HEADS_PER_STEP + a, 1), pl.ds(start, t)]
            s = jnp.dot(q, kt, preferred_element_type=F32) - ck
            if masked:
                r = lax.broadcasted_iota(jnp.int32, (t, t), 0)
                c = lax.broadcasted_iota(jnp.int32, (t, t), 1)
                s = jnp.where(r >= c, s, NEG_BIG)
            m_old = m_ref[a]
            m_new = jnp.maximum(m_old, jnp.max(s, axis=1, keepdims=True))
            alpha = jnp.exp2(m_old - m_new)
            p = jnp.exp2(s - m_new)
            l_ref[a] = alpha * l_ref[a] + jnp.sum(p, axis=1, keepdims=True)
            acc_ref[a] = alpha * acc_ref[a] + jnp.dot(p.astype(BF16), v,
                                                      preferred_element_type=F32)
            m_ref[a] = m_new

    def body(ki, carry):
        step(ki, False)
        return carry

    lax.fori_loop(0, qi, body, 0)
    step(qi, True)
    for a in range(HEADS_PER_STEP):
        o_ref[0, :, a * HEAD_DIM:(a + 1) * HEAD_DIM] = acc_ref[a] / l_ref[a]


def _flash_prompt(q, kb_t, vb, c_row):
    b, n, _ = q.shape
    t = ATT_TILE
    w = HEADS_PER_STEP * HEAD_DIM
    return pl.pallas_call(
        _flash_kernel,
        grid=(b, N_HEADS // HEADS_PER_STEP, n // t),
        in_specs=[
            pl.BlockSpec((1, t, w), lambda bi, h, qi: (bi, qi, h)),
            pl.BlockSpec((1, w, n), lambda bi, h, qi: (bi, h, 0)),
            pl.BlockSpec((1, n, w), lambda bi, h, qi: (bi, 0, h)),
            pl.BlockSpec((1, N_HEADS, n), lambda bi, h, qi: (bi, 0, 0)),
        ],
        out_specs=pl.BlockSpec((1, t, w), lambda bi, h, qi: (bi, qi, h)),
        out_shape=jax.ShapeDtypeStruct((b, n, D_ATTN), F32),
        scratch_shapes=[pltpu.VMEM((HEADS_PER_STEP, t, 1), F32),
                        pltpu.VMEM((HEADS_PER_STEP, t, 1), F32),
                        pltpu.VMEM((HEADS_PER_STEP, t, HEAD_DIM), F32)],
        compiler_params=_cparams(("parallel", "parallel", "arbitrary")),
        name="flash_prompt",
    )(q, kb_t, vb, c_row)


def _page_decay_kernel(lf_ref, o_ref):
    n_pages, heads, page = lf_ref.shape[1:]
    r = lax.broadcasted_iota(jnp.int32, (page, page), 0)
    c = lax.broadcasted_iota(jnp.int32, (page, page), 1)
    later = (r > c).astype(F32)
    x = lf_ref[0].reshape(n_pages * heads, page)
    suffix = jnp.dot(x, later, precision=lax.Precision.HIGHEST, preferred_element_type=F32)
    total = jnp.sum(x, axis=1, keepdims=True)
    o_ref[0, 0] = suffix.reshape(n_pages, heads, page)
    o_ref[0, 1] = jnp.broadcast_to(total, x.shape).reshape(n_pages, heads, page)


def _page_decay(cache_lf_t):
    depth, pool, heads, page = cache_lf_t.shape
    pb = DECAY_PAGES
    return pl.pallas_call(
        _page_decay_kernel,
        grid=(depth, pool // pb),
        in_specs=[pl.BlockSpec((1, pb, heads, page), lambda d, i: (d, i, 0, 0))],
        out_specs=pl.BlockSpec((1, 2, pb, heads, page), lambda d, i: (d, 0, i, 0, 0)),
        out_shape=jax.ShapeDtypeStruct((depth, 2, pool, heads, page), F32),
        compiler_params=_cparams(("parallel", "parallel")),
        name="page_decay",
    )(cache_lf_t)


def _sample_attn_kernel(pt_ref, q_ref, kn_ref, vn_ref, lfc_ref, *rest):
    g = PAGES_PER_STEP
    k_refs = rest[0:g]
    v_refs = rest[g:2 * g]
    d_refs = rest[2 * g:3 * g]
    o_ref = rest[3 * g]
    m_ref, l_ref, acc_ref, carry_ref = rest[3 * g + 1:]
    s_idx = pl.program_id(1)
    rows = q_ref.shape[1]
    page = k_refs[0].shape[2]
    keys = page * N_HEADS

    @pl.when(s_idx == 0)
    def _():
        m_ref[...] = jnp.full_like(m_ref, NEG_BIG)
        l_ref[...] = jnp.zeros_like(l_ref)
        acc_ref[...] = jnp.zeros_like(acc_ref)
        carry_ref[...] = jnp.zeros_like(carry_ref)

    qb = q_ref[0].astype(BF16)
    row_head = lax.broadcasted_iota(jnp.int32, (rows, keys), 0) % N_HEADS
    key_head = lax.broadcasted_iota(jnp.int32, (rows, keys), 1) % N_HEADS
    head_match = row_head == key_head

    def online_update(scores, values):
        m_old = m_ref[...]
        m_new = m_old
        for s in scores:
            m_new = jnp.maximum(m_new, jnp.max(s, axis=1, keepdims=True))
        alpha = jnp.exp2(m_old - m_new)
        l_new = alpha * l_ref[...]
        acc = alpha * acc_ref[...]
        for s, v_bf in zip(scores, values):
            p = jnp.exp2(s - m_new)
            l_new = l_new + jnp.sum(p, axis=1, keepdims=True)
            acc = acc + jnp.dot(p.astype(BF16), v_bf, preferred_element_type=F32)
        l_ref[...] = l_new
        acc_ref[...] = acc
        m_ref[...] = m_new

    carry = carry_ref[...]
    scores = []
    for i in range(g):
        kb = k_refs[i][0, 0].reshape(keys, HEAD_DIM).astype(BF16)
        bias = (d_refs[i][0, 0, 0] + carry) * LOG2E
        carry = carry + d_refs[i][0, 0, 1]
        s = lax.dot_general(qb, kb, (((1,), (1,)), ((), ())), preferred_element_type=F32)
        scores.append(jnp.where(head_match, s + bias, NEG_BIG))
    carry_ref[...] = carry
    values = [v_refs[i][0, 0].reshape(keys, HEAD_DIM).astype(BF16) for i in range(g)]
    online_update(scores, values)

    @pl.when(s_idx == pl.num_programs(1) - 1)
    def _():
        kn = kn_ref[0].astype(BF16)
        vn = vn_ref[0].astype(BF16)
        s = lax.dot_general(qb, kn, (((1,), (1,)), ((), ())), preferred_element_type=F32)
        r = lax.broadcasted_iota(jnp.int32, (rows, rows), 0)
        c = lax.broadcasted_iota(jnp.int32, (rows, rows), 1)
        same_head = (r % N_HEADS) == (c % N_HEADS)
        upto = jnp.where(same_head & (r <= c), lfc_ref[0], 0.0)
        cn_row = jnp.sum(upto, axis=0, keepdims=True)
        s = jnp.where(same_head & (c <= r), s - cn_row * LOG2E, NEG_BIG)
        online_update([s], [vn])
        o_ref[0] = acc_ref[...] / l_ref[...]


def _sample_attn(q, k_new, v_new, lf_new, cache_k, cache_v, decay, page_table, layer):
    n, rows, _ = q.shape
    n_pages = page_table.shape[1]
    page = cache_k.shape[2]
    keys = page * N_HEADS
    g = PAGES_PER_STEP
    steps = n_pages // g

    def page_map5(i):
        return lambda b, s, pt: (layer, pt[b, n_pages - 1 - (s * g + i)], 0, 0, 0)

    per_seq = lambda b, s, pt: (b, 0, 0)
    in_specs = [
        pl.BlockSpec((1, rows, HEAD_DIM), per_seq),
        pl.BlockSpec((1, rows, HEAD_DIM), per_seq),
        pl.BlockSpec((1, rows, HEAD_DIM), per_seq),
        pl.BlockSpec((1, rows, 1), per_seq),
    ]
    in_specs += [pl.BlockSpec((1, 1, page, N_HEADS, HEAD_DIM), page_map5(i)) for i in range(g)]
    in_specs += [pl.BlockSpec((1, 1, page, N_HEADS, HEAD_DIM), page_map5(i)) for i in range(g)]
    in_specs += [pl.BlockSpec((1, 1, 2, 1, keys), page_map5(i)) for i in range(g)]
    grid_spec = pltpu.PrefetchScalarGridSpec(
        num_scalar_prefetch=1,
        grid=(n, steps),
        in_specs=in_specs,
        out_specs=pl.BlockSpec((1, rows, HEAD_DIM), per_seq),
        scratch_shapes=[pltpu.VMEM((rows, 1), F32), pltpu.VMEM((rows, 1), F32),
                        pltpu.VMEM((rows, HEAD_DIM), F32), pltpu.VMEM((1, keys), F32)],
    )
    return pl.pallas_call(
        _sample_attn_kernel,
        grid_spec=grid_spec,
        out_shape=jax.ShapeDtypeStruct((n, rows, HEAD_DIM), F32),
        compiler_params=_cparams(("parallel", "arbitrary")),
        name="sample_attn",
    )(page_table, q, k_new, v_new, lf_new.reshape(n, rows, 1),
      *([cache_k] * g), *([cache_v] * g), *([decay] * g))


def _discretise(a_re, a_im, log_dt):
    dt = jnp.exp(log_dt)
    mag = jnp.exp(dt * a_re)
    ab_re = mag * jnp.cos(dt * a_im)
    ab_im = mag * jnp.sin(dt * a_im)
    den = a_re * a_re + a_im * a_im
    f_re = ((ab_re - 1.0) * a_re + ab_im * a_im) / den
    f_im = (ab_im * a_re - (ab_re - 1.0) * a_im) / den
    return ab_re, ab_im, f_re, f_im


def _s5_prep_kernel(pe_ref, b_ref, pf_ref, c_ref, ps_ref,
                    e_ref, f_ref, m_ref, pw_ref, e32_ref, f0_ref):
    t_len = S5_CHUNK
    sc = STATE_COLS
    ab_re, ab_im, f_re, f_im = _discretise(pe_ref[0, 0], pe_ref[0, 1], pe_ref[0, 2])
    b_re = b_ref[0, 0]
    b_im = b_ref[0, 1]
    row_g = lax.broadcasted_iota(jnp.int32, (LANES, sc), 0) // SSM_GROUP
    col_g = lax.broadcasted_iota(jnp.int32, (LANES, sc), 1) // SSM_STATE
    same = row_g == col_g
    w_re = jnp.where(same, f_re * b_re - f_im * b_im, 0.0)
    w_im = jnp.where(same, f_re * b_im + f_im * b_re, 0.0)
    for j in range(t_len):
        s = t_len - 1 - j
        e32_ref[s * LANES:(s + 1) * LANES, 0:sc] = w_re
        e32_ref[s * LANES:(s + 1) * LANES, sc:2 * sc] = w_im
        w_re, w_im = ab_re * w_re - ab_im * w_im, ab_re * w_im + ab_im * w_re
    e_ref[0] = e32_ref[...].astype(BF16)

    ab_re, ab_im, _, _ = _discretise(pf_ref[0, 0], pf_ref[0, 1], pf_ref[0, 2])
    row_g = lax.broadcasted_iota(jnp.int32, (sc, LANES), 0) // SSM_STATE
    col_g = lax.broadcasted_iota(jnp.int32, (sc, LANES), 1) // SSM_GROUP
    same = row_g == col_g
    g_re = jnp.where(same, c_ref[0, 0], 0.0)
    g_im = jnp.where(same, c_ref[0, 1], 0.0)
    f0_ref[0:sc, :] = g_re
    f0_ref[sc:2 * sc, :] = -g_im
    for t in range(t_len):
        g_re, g_im = g_re * ab_re - g_im * ab_im, g_re * ab_im + g_im * ab_re
        f_ref[0, 0:sc, t * LANES:(t + 1) * LANES] = g_re.astype(BF16)
        f_ref[0, sc:2 * sc, t * LANES:(t + 1) * LANES] = (-g_im).astype(BF16)

    lag = jnp.dot(e32_ref[...], f0_ref[...], precision=lax.Precision.HIGHEST,
                  preferred_element_type=F32).astype(BF16)
    m_ref[0] = jnp.zeros(m_ref.shape[1:], BF16)
    for t in range(t_len):
        m_ref[0, 0:(t + 1) * LANES, t * LANES:(t + 1) * LANES] = lag[(t_len - 1 - t) * LANES:, :]

    a_re, a_im, _, _ = _discretise(ps_ref[0, 0:1], ps_ref[0, 1:2], ps_ref[0, 2:3])
    a2_re, a2_im = a_re * a_re - a_im * a_im, 2.0 * a_re * a_im
    a4_re, a4_im = a2_re * a2_re - a2_im * a2_im, 2.0 * a2_re * a2_im
    a8_re, a8_im = a4_re * a4_re - a4_im * a4_im, 2.0 * a4_re * a4_im
    a16_re, a16_im = a8_re * a8_re - a8_im * a8_im, 2.0 * a8_re * a8_im
    pw_ref[0, 0:1, 0:sc] = a4_re
    pw_ref[0, 0:1, sc:2 * sc] = a4_im
    pw_ref[0, 1:2, 0:sc] = a16_re
    pw_ref[0, 1:2, sc:2 * sc] = a16_im


def _s5_prep(a_re, a_im, log_dt, b_re, b_im, c_re, c_im):
    nb, gb, st, ch = N_LANE_BLOCKS, GROUPS_PER_BLOCK, SSM_STATE, SSM_GROUP
    ldt = jnp.broadcast_to(log_dt[:, None], (SSM_GROUPS, st))
    par = jnp.stack([a_re, a_im, ldt], axis=0).reshape(3, nb, gb, st)
    pe = jnp.broadcast_to(par[:, :, :, None, None, :], (3, nb, gb, ch, gb, st))
    pe = pe.reshape(3, nb, LANES, STATE_COLS).transpose(1, 0, 2, 3)
    bb = jnp.stack([b_re, b_im], axis=0).reshape(2, nb, gb, st, ch).transpose(0, 1, 2, 4, 3)
    bb = jnp.broadcast_to(bb[:, :, :, :, None, :], (2, nb, gb, ch, gb, st))
    bb = bb.reshape(2, nb, LANES, STATE_COLS).transpose(1, 0, 2, 3)
    pf = jnp.broadcast_to(par[:, :, :, :, None, None], (3, nb, gb, st, gb, ch))
    pf = pf.reshape(3, nb, STATE_COLS, LANES).transpose(1, 0, 2, 3)
    cc = jnp.stack([c_re, c_im], axis=0).reshape(2, nb, gb, ch, st).transpose(0, 1, 2, 4, 3)
    cc = jnp.broadcast_to(cc[:, :, :, :, None, :], (2, nb, gb, st, gb, ch))
    cc = cc.reshape(2, nb, STATE_COLS, LANES).transpose(1, 0, 2, 3)
    ps = par.reshape(3, nb, STATE_COLS).transpose(1, 0, 2)

    tl = S5_CHUNK * LANES
    blk4 = lambda i: (i, 0, 0, 0)
    blk3 = lambda i: (i, 0, 0)
    return pl.pallas_call(
        _s5_prep_kernel,
        grid=(nb,),
        in_specs=[
            pl.BlockSpec((1, 3, LANES, STATE_COLS), blk4),
            pl.BlockSpec((1, 2, LANES, STATE_COLS), blk4),
            pl.BlockSpec((1, 3, STATE_COLS, LANES), blk4),
            pl.BlockSpec((1, 2, STATE_COLS, LANES), blk4),
            pl.BlockSpec((1, 3, STATE_COLS), blk3),
        ],
        out_specs=(
            pl.BlockSpec((1, tl, 2 * STATE_COLS), blk3),
            pl.BlockSpec((1, 2 * STATE_COLS, tl), blk3),
            pl.BlockSpec((1, tl, tl), blk3),
            pl.BlockSpec((1, 2, 2 * STATE_COLS), blk3),
        ),
        out_shape=(
            jax.ShapeDtypeStruct((nb, tl, 2 * STATE_COLS), BF16),
            jax.ShapeDtypeStruct((nb, 2 * STATE_COLS, tl), BF16),
            jax.ShapeDtypeStruct((nb, tl, tl), BF16),
            jax.ShapeDtypeStruct((nb, 2, 2 * STATE_COLS), F32),
        ),
        scratch_shapes=[pltpu.VMEM((tl, 2 * STATE_COLS), F32),
                        pltpu.VMEM((2 * STATE_COLS, LANES), F32)],
        compiler_params=_cparams(("parallel",)),
        name="s5_prep",
    )(pe, bb, pf, cc, ps)


def _s5_prompt_kernel(u_ref, e_ref, m_ref, f_ref, pw_ref, d_ref,
                      y_ref, hl_ref, ub_ref, sloc_ref, hp_ref, hpb_ref):
    ct = pl.program_id(1)
    n_rows = u_ref.shape[1]
    n_batch = hl_ref.shape[1]
    per_batch = n_rows // n_batch
    sc = STATE_COLS

    @pl.when(ct == 0)
    def _():
        ub = u_ref[0].astype(BF16)
        ub_ref[...] = ub
        sloc_ref[...] = jnp.dot(ub, e_ref[0], preferred_element_type=F32)
        p_re = pw_ref[0, 1:2, 0:sc]
        p_im = pw_ref[0, 1:2, sc:2 * sc]

        def body(k, carry):
            new = []
            for b in range(n_batch):
                h_re, h_im = carry[2 * b], carry[2 * b + 1]
                row = b * per_batch + k
                hp_ref[pl.ds(row, 1), 0:sc] = h_re
                hp_ref[pl.ds(row, 1), sc:2 * sc] = h_im
                s_re = sloc_ref[pl.ds(row, 1), 0:sc]
                s_im = sloc_ref[pl.ds(row, 1), sc:2 * sc]
                new.append(p_re * h_re - p_im * h_im + s_re)
                new.append(p_re * h_im + p_im * h_re + s_im)
            return tuple(new)

        zero = jnp.zeros((1, sc), F32)
        final = lax.fori_loop(0, per_batch, body, (zero,) * (2 * n_batch))
        for b in range(n_batch):
            hl_ref[0, b:b + 1, 0:sc] = final[2 * b]
            hl_ref[0, b:b + 1, sc:2 * sc] = final[2 * b + 1]
        hpb_ref[...] = hp_ref[...].astype(BF16)

    w = y_ref.shape[2]
    col = pl.multiple_of(ct * w, w)
    y = jnp.dot(ub_ref[...], m_ref[0], preferred_element_type=F32)
    y = y + jnp.dot(hpb_ref[...], f_ref[0], preferred_element_type=F32)
    y_ref[0] = y + d_ref[0] * u_ref[0, :, pl.ds(col, w)]


def _s5_prompt(u_chunks, e_op, m_op, f_op, pw, d_tiled, n_batch):
    nb, n_rows, tl = u_chunks.shape
    w = 512
    return pl.pallas_call(
        _s5_prompt_kernel,
        grid=(nb, tl // w),
        in_specs=[
            pl.BlockSpec((1, n_rows, tl), lambda i, c: (i, 0, 0)),
            pl.BlockSpec((1, tl, 2 * STATE_COLS), lambda i, c: (i, 0, 0)),
            pl.BlockSpec((1, tl, w), lambda i, c: (i, 0, c)),
            pl.BlockSpec((1, 2 * STATE_COLS, w), lambda i, c: (i, 0, c)),
            pl.BlockSpec((1, 2, 2 * STATE_COLS), lambda i, c: (i, 0, 0)),
            pl.BlockSpec((1, 1, w), lambda i, c: (i, 0, 0)),
        ],
        out_specs=(
            pl.BlockSpec((1, n_rows, w), lambda i, c: (i, 0, c)),
            pl.BlockSpec((1, n_batch, 2 * STATE_COLS), lambda i, c: (i, 0, 0)),
        ),
        out_shape=(
            jax.ShapeDtypeStruct((nb, n_rows, tl), F32),
            jax.ShapeDtypeStruct((nb, n_batch, 2 * STATE_COLS), F32),
        ),
        scratch_shapes=[pltpu.VMEM((n_rows, tl), BF16),
                        pltpu.VMEM((n_rows, 2 * STATE_COLS), F32),
                        pltpu.VMEM((n_rows, 2 * STATE_COLS), F32),
                        pltpu.VMEM((n_rows, 2 * STATE_COLS), BF16)],
        compiler_params=_cparams(("parallel", "arbitrary")),
        name="s5_prompt",
    )(u_chunks, e_op, m_op, f_op, pw, d_tiled)


def _s5_sample_kernel(u_ref, e_ref, m_ref, f_ref, pw_ref, d_ref, h0_ref, y_ref, hl_ref):
    sc = STATE_COLS
    u = u_ref[0]
    ub = u.astype(BF16)
    h0 = h0_ref[0]
    y = jnp.dot(ub, m_ref[0], preferred_element_type=F32)
    y = y + jnp.dot(h0.astype(BF16), f_ref[0], preferred_element_type=F32)
    y_ref[0] = y + d_ref[0] * u
    sloc = jnp.dot(ub, e_ref[0], preferred_element_type=F32)
    p_re = pw_ref[0, 0:1, 0:sc]
    p_im = pw_ref[0, 0:1, sc:2 * sc]
    h_re = h0[:, 0:sc]
    h_im = h0[:, sc:2 * sc]
    hl_ref[0, :, 0:sc] = p_re * h_re - p_im * h_im + sloc[:, 0:sc]
    hl_ref[0, :, sc:2 * sc] = p_re * h_im + p_im * h_re + sloc[:, sc:2 * sc]


def _s5_sample(u_chunks, e_op, m_op, f_op, pw, d_tiled, h0):
    nb, n_seq, w = u_chunks.shape
    tl = e_op.shape[1]
    e_blk = (tl - w) // w
    return pl.pallas_call(
        _s5_sample_kernel,
        grid=(nb,),
        in_specs=[
            pl.BlockSpec((1, n_seq, w), lambda i: (i, 0, 0)),
            pl.BlockSpec((1, w, 2 * STATE_COLS), lambda i: (i, e_blk, 0)),
            pl.BlockSpec((1, w, w), lambda i: (i, 0, 0)),
            pl.BlockSpec((1, 2 * STATE_COLS, w), lambda i: (i, 0, 0)),
            pl.BlockSpec((1, 2, 2 * STATE_COLS), lambda i: (i, 0, 0)),
            pl.BlockSpec((1, 1, w), lambda i: (i, 0, 0)),
            pl.BlockSpec((1, n_seq, 2 * STATE_COLS), lambda i: (i, 0, 0)),
        ],
        out_specs=(
            pl.BlockSpec((1, n_seq, w), lambda i: (i, 0, 0)),
            pl.BlockSpec((1, n_seq, 2 * STATE_COLS), lambda i: (i, 0, 0)),
        ),
        out_shape=(
            jax.ShapeDtypeStruct((nb, n_seq, w), F32),
            jax.ShapeDtypeStruct((nb, n_seq, 2 * STATE_COLS), F32),
        ),
        compiler_params=_cparams(("parallel",)),
        name="s5_sample",
    )(u_chunks, e_op, m_op, f_op, pw, d_tiled, h0)


def _mix_out_kernel(x_ref, attn_ref, y_ref, gw_ref, gb_ref, ga_ref, gs_ref, wo_ref, gp_ref,
                    o_ref, ycat_ref, mg_ref):
    for lb in range(N_LANE_BLOCKS):
        ycat_ref[:, lb * LANES:(lb + 1) * LANES] = jax.nn.gelu(y_ref[lb])
    y = ycat_ref[...]
    z = jnp.dot(y.astype(BF16), gw_ref[...], preferred_element_type=F32) + gb_ref[...]
    s = y * jax.nn.sigmoid(z)
    mg_ref[:, 0:D_ATTN] = _rms(attn_ref[...], ga_ref[...]).astype(BF16)
    mg_ref[:, D_ATTN:] = _rms(s, gs_ref[...]).astype(BF16)
    m = jnp.dot(mg_ref[...], wo_ref[...], preferred_element_type=F32)
    o_ref[...] = x_ref[...] + _rms(m, gp_ref[...])


def _mix_out(x, attn, y_blk, glu_w, glu_b, g_attn, g_ssm, w_out, g_post, layer):
    m = x.shape[0]
    tm = 256
    row = lambda i: (i, 0)
    const = lambda i: (0, 0)
    per_layer = lambda i: (layer, 0, 0)
    return pl.pallas_call(
        _mix_out_kernel,
        grid=(m // tm,),
        in_specs=[
            pl.BlockSpec((tm, D_MODEL), row),
            pl.BlockSpec((tm, D_ATTN), row),
            pl.BlockSpec((N_LANE_BLOCKS, tm, LANES), lambda i: (0, i, 0)),
            pl.BlockSpec((None, D_SSM, D_SSM), per_layer),
            pl.BlockSpec((1, D_SSM), const),
            pl.BlockSpec((1, D_ATTN), const),
            pl.BlockSpec((1, D_SSM), const),
            pl.BlockSpec((None, D_MODEL, D_MODEL), per_layer),
            pl.BlockSpec((1, D_MODEL), const),
        ],
        out_specs=pl.BlockSpec((tm, D_MODEL), row),
        out_shape=jax.ShapeDtypeStruct((m, D_MODEL), F32),
        scratch_shapes=[pltpu.VMEM((tm, D_SSM), F32), pltpu.VMEM((tm, D_MODEL), BF16)],
        compiler_params=_cparams(("parallel",)),
        name="mix_out",
    )(x, attn, y_blk, glu_w, glu_b, g_attn, g_ssm, w_out, g_post)


def _state_to_blocks(h):
    n = h.shape[0]
    h = h.reshape(n, N_LANE_BLOCKS, STATE_COLS, 2)
    return h.transpose(1, 0, 3, 2).reshape(N_LANE_BLOCKS, n, 2 * STATE_COLS)


def _blocks_to_state(h):
    n = h.shape[1]
    h = h.reshape(N_LANE_BLOCKS, n, 2, STATE_COLS).transpose(1, 0, 3, 2)
    return h.reshape(n, SSM_GROUPS, SSM_STATE, 2)


def _trunk_layer(x, l, w, mixer):
    ng = w["ng"][l]
    x = _ffn_half(x, ng[0], ng[1], w["wg"], w["wu"], w["wd"], l, 0)
    u_blk, q, k, v, kb, vb, lf = _mix_in(x, ng[2], w["w_main"], w["w_f"], w["b_f"], l)
    logf = lf[:, :N_HEADS]
    attn, y_blk, h_last = mixer(u_blk, q, k, v, kb, vb, logf)
    x = _mix_out(x, attn, y_blk, w["glu_w"], w["glu_b"][l], w["g_attn"][l], w["g_ssm"][l],
                 w["w_out"], ng[3], l)
    x = _ffn_half(x, ng[4], ng[5], w["wg"], w["wu"], w["wd"], l, 1)
    return x, k, v, logf, h_last


def kernel(x_prompt, x_sample, cache_k, cache_v, cache_logf, state_ssm, page_table, norm_g, ffn_gate, ffn_up, ffn_down, w_in, b_forget, ssm_a_re, ssm_a_im, ssm_log_dt, ssm_b_re, ssm_b_im, ssm_c_re, ssm_c_im, ssm_d, glu_w, glu_b, g_attn_out, g_ssm_out, w_out):
    batch, seq, _ = x_prompt.shape
    n_seq, n_new, _ = x_sample.shape
    depth = norm_g.shape[0]
    n_pool, page = cache_k.shape[1], cache_k.shape[2]
    assert seq % ROW_TILE == 0 and seq % ATT_TILE == 0 and seq % S5_CHUNK == 0
    assert (n_seq * n_new) % 256 == 0 and page_table.shape[1] % PAGES_PER_STEP == 0
    assert n_pool % DECAY_PAGES == 0 and N_HEADS == SUBLANES

    xp = x_prompt.reshape(batch * seq, D_MODEL)
    xs = x_sample.reshape(n_seq * n_new, D_MODEL)

    ff_pad = D_FF_PAD - D_FF
    n_main = D_SSM + 3 * D_ATTN
    w = {
        "ng": [[norm_g[l, i].reshape(1, D_MODEL) for i in range(6)] for l in range(depth)],
        "wg": jnp.pad(ffn_gate, ((0, 0), (0, 0), (0, 0), (0, ff_pad))).astype(BF16),
        "wu": jnp.pad(ffn_up, ((0, 0), (0, 0), (0, 0), (0, ff_pad))).astype(BF16),
        "wd": jnp.pad(ffn_down, ((0, 0), (0, 0), (0, ff_pad), (0, 0))).astype(BF16),
        "w_main": w_in[:, :, :n_main].astype(BF16),
        "w_f": jnp.pad(w_in[:, :, n_main:], ((0, 0), (0, 0), (0, LANES - N_HEADS))).astype(BF16),
        "b_f": jnp.pad(b_forget, ((0, 0), (0, LANES - N_HEADS))).reshape(depth, 1, LANES),
        "glu_w": glu_w.astype(BF16),
        "glu_b": [glu_b[l].reshape(1, D_SSM) for l in range(depth)],
        "g_attn": [g_attn_out[l].reshape(1, D_ATTN) for l in range(depth)],
        "g_ssm": [g_ssm_out[l].reshape(1, D_SSM) for l in range(depth)],
        "w_out": w_out.astype(BF16),
    }

    decay = _page_decay(cache_logf.transpose(0, 1, 3, 2))
    decay = decay.transpose(0, 2, 1, 4, 3).reshape(depth, n_pool, 2, 1, page * N_HEADS)

    outs = {name: [] for name in ("kp", "vp", "lfp", "sp", "ks", "vs", "lfs", "ss")}
    for l in range(depth):
        e_op, f_op, m_op, pw = _s5_prep(ssm_a_re[l], ssm_a_im[l], ssm_log_dt[l],
                                        ssm_b_re[l], ssm_b_im[l], ssm_c_re[l], ssm_c_im[l])
        d_blk = ssm_d[l].reshape(N_LANE_BLOCKS, 1, LANES)
        d_prompt = jnp.tile(d_blk, (1, 1, 512 // LANES))
        d_sample = jnp.tile(d_blk, (1, 1, n_new))

        def prompt_mixer(u_blk, q, k, v, kb, vb, logf):
            lf_t = logf.reshape(batch, seq, N_HEADS).transpose(0, 2, 1)
            c_row = _cumsum_heads(lf_t)
            attn = _flash_prompt(q.reshape(batch, seq, D_ATTN),
                                 kb.reshape(batch, seq, D_ATTN).transpose(0, 2, 1),
                                 vb.reshape(batch, seq, D_ATTN), c_row)
            u_chunks = u_blk.reshape(N_LANE_BLOCKS, batch * seq // S5_CHUNK, S5_CHUNK * LANES)
            y, h_last = _s5_prompt(u_chunks, e_op, m_op, f_op, pw, d_prompt, batch)
            return (attn.reshape(batch * seq, D_ATTN),
                    y.reshape(N_LANE_BLOCKS, batch * seq, LANES), _blocks_to_state(h_last))

        def sample_mixer(u_blk, q, k, v, kb, vb, logf):
            rows = n_new * N_HEADS
            attn = _sample_attn(q.astype(F32).reshape(n_seq, rows, HEAD_DIM),
                                k.reshape(n_seq, rows, HEAD_DIM), v.reshape(n_seq, rows, HEAD_DIM),
                                logf.reshape(n_seq, rows), cache_k, cache_v, decay,
                                page_table, l)
            u_chunks = u_blk.reshape(N_LANE_BLOCKS, n_seq, n_new * LANES)
            y, h_last = _s5_sample(u_chunks, e_op, m_op, f_op, pw, d_sample,
                                   _state_to_blocks(state_ssm[l]))
            return (attn.reshape(n_seq * n_new, D_ATTN),
                    y.reshape(N_LANE_BLOCKS, n_seq * n_new, LANES), _blocks_to_state(h_last))

        xp, k1, v1, lf1, s1 = _trunk_layer(xp, l, w, prompt_mixer)
        xs, k2, v2, lf2, s2 = _trunk_layer(xs, l, w, sample_mixer)
        outs["kp"].append(k1.reshape(batch, seq, N_HEADS, HEAD_DIM))
        outs["vp"].append(v1.reshape(batch, seq, N_HEADS, HEAD_DIM))
        outs["lfp"].append(lf1.reshape(batch, seq, N_HEADS))
        outs["sp"].append(s1)
        outs["ks"].append(k2.reshape(n_seq, n_new, N_HEADS, HEAD_DIM))
        outs["vs"].append(v2.reshape(n_seq, n_new, N_HEADS, HEAD_DIM))
        outs["lfs"].append(lf2.reshape(n_seq, n_new, N_HEADS))
        outs["ss"].append(s2)

    return (xp.reshape(batch, seq, D_MODEL), xs.reshape(n_seq, n_new, D_MODEL),
            jnp.stack(outs["kp"]), jnp.stack(outs["vp"]), jnp.stack(outs["lfp"]),
            jnp.stack(outs["sp"]),
            jnp.stack(outs["ks"]), jnp.stack(outs["vs"]), jnp.stack(outs["lfs"]),
            jnp.stack(outs["ss"]))
```

```python
import math

import jax
import jax.numpy as jnp
from jax import lax
from jax.experimental import pallas as pl
from jax.experimental.pallas import tpu as pltpu

F32 = jnp.float32
BF16 = jnp.bfloat16

D_MODEL = 2048
D_SSM = 1024
D_ATTN = 1024
HEAD_DIM = 128
N_HEADS = 8
SSM_GROUP = 16
SSM_GROUPS = 64
SSM_STATE = 64
D_FF = 5504
EPS = 1e-6
LOG2E = math.log2(math.e)

LANES = 128
SUBLANES = 8
GROUPS_PER_BLOCK = LANES // SSM_GROUP
N_LANE_BLOCKS = D_SSM // LANES
STATE_COLS = GROUPS_PER_BLOCK * SSM_STATE
S5_CHUNK = 16
FF_TILE = 512
D_FF_PAD = ((D_FF + FF_TILE - 1) // FF_TILE) * FF_TILE
ROW_TILE = 512
ATT_TILE = 512
ATT_COLS = 512
HEADS_PER_STEP = 2
PAGES_PER_STEP = 16
DECAY_PAGES = 256
VMEM_LIMIT = 56 * 1024 * 1024
NEG_BIG = -1e30


def _cparams(sem):
    return pltpu.CompilerParams(dimension_semantics=sem, vmem_limit_bytes=VMEM_LIMIT)


def _rms(x, g):
    ms = jnp.mean(x * x, axis=-1, keepdims=True)
    return x * lax.rsqrt(ms + EPS) * g


def _log_sigmoid(z):
    return jnp.minimum(z, 0.0) - jnp.log1p(jnp.exp(-jnp.abs(z)))


def _ffn_kernel(x_ref, gpre_ref, gpost_ref, wg_ref, wu_ref, wd_ref, o_ref, h_ref, acc_ref):
    j = pl.program_id(1)

    @pl.when(j == 0)
    def _():
        h_ref[...] = _rms(x_ref[...], gpre_ref[...]).astype(BF16)
        acc_ref[...] = jnp.zeros_like(acc_ref)

    h = h_ref[...]
    g = jnp.dot(h, wg_ref[...], preferred_element_type=F32)
    u = jnp.dot(h, wu_ref[...], preferred_element_type=F32)
    a = (g * jax.nn.sigmoid(g) * u).astype(BF16)
    acc_ref[...] += jnp.dot(a, wd_ref[...], preferred_element_type=F32)

    @pl.when(j == pl.num_programs(1) - 1)
    def _():
        o_ref[...] = x_ref[...] + 0.5 * _rms(acc_ref[...], gpost_ref[...])


def _ffn_half(x, g_pre, g_post, wg, wu, wd, layer, half):
    m = x.shape[0]
    tm = min(ROW_TILE, m)
    grid = (m // tm, D_FF_PAD // FF_TILE)
    return pl.pallas_call(
        _ffn_kernel,
        grid=grid,
        in_specs=[
            pl.BlockSpec((tm, D_MODEL), lambda i, j: (i, 0)),
            pl.BlockSpec((1, D_MODEL), lambda i, j: (0, 0)),
            pl.BlockSpec((1, D_MODEL), lambda i, j: (0, 0)),
            pl.BlockSpec((None, None, D_MODEL, FF_TILE), lambda i, j: (layer, half, 0, j)),
            pl.BlockSpec((None, None, D_MODEL, FF_TILE), lambda i, j: (layer, half, 0, j)),
            pl.BlockSpec((None, None, FF_TILE, D_MODEL), lambda i, j: (layer, half, j, 0)),
        ],
        out_specs=pl.BlockSpec((tm, D_MODEL), lambda i, j: (i, 0)),
        out_shape=jax.ShapeDtypeStruct((m, D_MODEL), F32),
        scratch_shapes=[pltpu.VMEM((tm, D_MODEL), BF16), pltpu.VMEM((tm, D_MODEL), F32)],
        compiler_params=_cparams(("parallel", "arbitrary")),
        name="ffn_half",
    )(x, g_pre, g_post, wg, wu, wd)


def _mix_in_kernel(x_ref, g_ref, w_ref, wf_ref, bf_ref,
                   u_ref, q_ref, k_ref, v_ref, kb_ref, vb_ref, lf_ref, h_ref):
    j = pl.program_id(1)

    @pl.when(j == 0)
    def _():
        h = _rms(x_ref[...], g_ref[...]).astype(BF16)
        h_ref[...] = h
        z = jnp.dot(h, wf_ref[...], preferred_element_type=F32) + bf_ref[...]
        lf_ref[...] = _log_sigmoid(z)

    p = jnp.dot(h_ref[...], w_ref[...], preferred_element_type=F32)

    @pl.when(j == 0)
    def _():
        for lb in range(N_LANE_BLOCKS):
            u_ref[lb] = p[:, lb * LANES:(lb + 1) * LANES]

    @pl.when(j == 1)
    def _():
        q_ref[...] = (p * (HEAD_DIM ** -0.5 * LOG2E)).astype(BF16)

    @pl.when(j == 2)
    def _():
        k_ref[...] = p
        kb_ref[...] = p.astype(BF16)

    @pl.when(j == 3)
    def _():
        v_ref[...] = p
        vb_ref[...] = p.astype(BF16)


def _mix_in(x, g, w_main, w_f, b_f, layer):
    m = x.shape[0]
    tm = min(ROW_TILE, m)
    row = lambda i, j: (i, 0)
    out_shape = (
        jax.ShapeDtypeStruct((N_LANE_BLOCKS, m, LANES), F32),
        jax.ShapeDtypeStruct((m, D_ATTN), BF16),
        jax.ShapeDtypeStruct((m, D_ATTN), F32),
        jax.ShapeDtypeStruct((m, D_ATTN), F32),
        jax.ShapeDtypeStruct((m, D_ATTN), BF16),
        jax.ShapeDtypeStruct((m, D_ATTN), BF16),
        jax.ShapeDtypeStruct((m, LANES), F32),
    )
    out_specs = (
        pl.BlockSpec((N_LANE_BLOCKS, tm, LANES), lambda i, j: (0, i, 0)),
        pl.BlockSpec((tm, D_ATTN), row),
        pl.BlockSpec((tm, D_ATTN), row),
        pl.BlockSpec((tm, D_ATTN), row),
        pl.BlockSpec((tm, D_ATTN), row),
        pl.BlockSpec((tm, D_ATTN), row),
        pl.BlockSpec((tm, LANES), row),
    )
    return pl.pallas_call(
        _mix_in_kernel,
        grid=(m // tm, 4),
        in_specs=[
            pl.BlockSpec((tm, D_MODEL), row),
            pl.BlockSpec((1, D_MODEL), lambda i, j: (0, 0)),
            pl.BlockSpec((None, D_MODEL, D_ATTN), lambda i, j: (layer, 0, j)),
            pl.BlockSpec((None, D_MODEL, LANES), lambda i, j: (layer, 0, 0)),
            pl.BlockSpec((None, 1, LANES), lambda i, j: (layer, 0, 0)),
        ],
        out_specs=out_specs,
        out_shape=out_shape,
        scratch_shapes=[pltpu.VMEM((tm, D_MODEL), BF16)],
        compiler_params=_cparams(("parallel", "arbitrary")),
        name="mix_in",
    )(x, g, w_main, w_f, b_f)


def _cumsum_kernel(x_ref, o_ref):
    blk = 512
    n = x_ref.shape[2]
    r = lax.broadcasted_iota(jnp.int32, (blk, blk), 0)
    c = lax.broadcasted_iota(jnp.int32, (blk, blk), 1)
    tri = (r <= c).astype(F32)
    carry = jnp.zeros((N_HEADS, 1), F32)
    for b in range(n // blk):
        xb = x_ref[0, :, b * blk:(b + 1) * blk]
        cb = jnp.dot(xb, tri, precision=lax.Precision.HIGHEST,
                     preferred_element_type=F32) + carry
        o_ref[0, :, b * blk:(b + 1) * blk] = cb * LOG2E
        carry = cb[:, blk - 1:blk]


def _cumsum_heads(lf_t):
    b, h, n = lf_t.shape
    return pl.pallas_call(
        _cumsum_kernel,
        grid=(b,),
        in_specs=[pl.BlockSpec((1, h, n), lambda i: (i, 0, 0))],
        out_specs=pl.BlockSpec((1, h, n), lambda i: (i, 0, 0)),
        out_shape=jax.ShapeDtypeStruct((b, h, n), F32),
        compiler_params=_cparams(("parallel",)),
        name="logf_cumsum",
    )(lf_t)


def _flash_kernel(q_ref, k_ref, v_ref, ccol_ref, o_ref, cb_ref, m_ref, l_ref, acc_ref):
    hb = pl.program_id(1)
    qi = pl.program_id(2)
    t = ATT_TILE
    n = k_ref.shape[1]

    @pl.when(qi == 0)
    def _():
        cc = ccol_ref[0]
        lane = lax.broadcasted_iota(jnp.int32, cc.shape, 1)
        for a in range(HEADS_PER_STEP):
            col = jnp.sum(jnp.where(lane == hb * HEADS_PER_STEP + a, cc, 0.0),
                          axis=1, keepdims=True)
            cb_ref[a] = jnp.broadcast_to(col, (n, LANES))

    m_ref[...] = jnp.full_like(m_ref, NEG_BIG)
    l_ref[...] = jnp.zeros_like(l_ref)
    acc_ref[...] = jnp.zeros_like(acc_ref)

    def step(ki, masked):
        start = pl.multiple_of(ki * t, t)
        for a in range(HEADS_PER_STEP):
            hl = slice(a * HEAD_DIM, (a + 1) * HEAD_DIM)
            k = k_ref[0, pl.ds(start, t), hl]
            v = v_ref[0, pl.ds(start, t), hl]
            cb = cb_ref[a, pl.ds(start, t), :]
            cb = jnp.concatenate([cb] * (ATT_COLS // LANES), axis=1)
            for j in range(t // ATT_COLS):
                cols = slice(j * ATT_COLS, (j + 1) * ATT_COLS)
                q = q_ref[0, cols, hl]
                s = lax.dot_general(k, q, (((1,), (1,)), ((), ())),
                                    preferred_element_type=F32) - cb
                if masked:
                    r = lax.broadcasted_iota(jnp.int32, s.shape, 0)
                    c = lax.broadcasted_iota(jnp.int32, s.shape, 1) + j * ATT_COLS
                    s = jnp.where(r <= c, s, NEG_BIG)
                m_old = m_ref[a, :, cols]
                m_new = jnp.maximum(m_old, jnp.max(s, axis=0, keepdims=True))
                alpha = jnp.exp2(m_old - m_new)
                p = jnp.exp2(s - m_new)
                l_ref[a, :, cols] = alpha * l_ref[a, :, cols] + jnp.sum(p, axis=0, keepdims=True)
                pv = lax.dot_general(v, p.astype(BF16), (((0,), (0,)), ((), ())),
                                     preferred_element_type=F32)
                acc_ref[a, :, cols] = alpha * acc_ref[a, :, cols] + pv
                m_ref[a, :, cols] = m_new

    def body(ki, carry):
        step(ki, False)
        return carry

    lax.fori_loop(0, qi, body, 0)
    step(qi, True)
    for a in range(HEADS_PER_STEP):
        o_ref[0, :, a * HEAD_DIM:(a + 1) * HEAD_DIM] = (acc_ref[a] / l_ref[a]).T


def _flash_prompt(q, kb, vb, c_col):
    b, n, _ = q.shape
    t = ATT_TILE
    w = HEADS_PER_STEP * HEAD_DIM
    return pl.pallas_call(
        _flash_kernel,
        grid=(b, N_HEADS // HEADS_PER_STEP, n // t),
        in_specs=[
            pl.BlockSpec((1, t, w), lambda bi, h, qi: (bi, qi, h)),
            pl.BlockSpec((1, n, w), lambda bi, h, qi: (bi, 0, h)),
            pl.BlockSpec((1, n, w), lambda bi, h, qi: (bi, 0, h)),
            pl.BlockSpec((1, n, N_HEADS), lambda bi, h, qi: (bi, 0, 0)),
        ],
        out_specs=pl.BlockSpec((1, t, w), lambda bi, h, qi: (bi, qi, h)),
        out_shape=jax.ShapeDtypeStruct((b, n, D_ATTN), F32),
        scratch_shapes=[pltpu.VMEM((HEADS_PER_STEP, n, LANES), F32),
                        pltpu.VMEM((HEADS_PER_STEP, 1, t), F32),
                        pltpu.VMEM((HEADS_PER_STEP, 1, t), F32),
                        pltpu.VMEM((HEADS_PER_STEP, HEAD_DIM, t), F32)],
        compiler_params=_cparams(("arbitrary", "arbitrary", "arbitrary")),
        name="flash_prompt",
    )(q, kb, vb, c_col)


def _page_decay_kernel(lf_ref, o_ref):
    n_pages, heads, page = lf_ref.shape[1:]
    r = lax.broadcasted_iota(jnp.int32, (page, page), 0)
    c = lax.broadcasted_iota(jnp.int32, (page, page), 1)
    later = (r > c).astype(F32)
    x = lf_ref[0].reshape(n_pages * heads, page)
    suffix = jnp.dot(x, later, precision=lax.Precision.HIGHEST, preferred_element_type=F32)
    total = jnp.sum(x, axis=1, keepdims=True)
    o_ref[0, 0] = suffix.reshape(n_pages, heads, page)
    o_ref[0, 1] = jnp.broadcast_to(total, x.shape).reshape(n_pages, heads, page)


def _page_decay(cache_lf_t):
    depth, pool, heads, page = cache_lf_t.shape
    pb = DECAY_PAGES
    return pl.pallas_call(
        _page_decay_kernel,
        grid=(depth, pool // pb),
        in_specs=[pl.BlockSpec((1, pb, heads, page), lambda d, i: (d, i, 0, 0))],
        out_specs=pl.BlockSpec((1, 2, pb, heads, page), lambda d, i: (d, 0, i, 0, 0)),
        out_shape=jax.ShapeDtypeStruct((depth, 2, pool, heads, page), F32),
        compiler_params=_cparams(("parallel", "parallel")),
        name="page_decay",
    )(cache_lf_t)


def _sample_attn_kernel(pt_ref, q_ref, kn_ref, vn_ref, lfc_ref, *rest):
    g = PAGES_PER_STEP
    k_refs = rest[0:g]
    v_refs = rest[g:2 * g]
    d_refs = rest[2 * g:3 * g]
    o_ref = rest[3 * g]
    m_ref, l_ref, acc_ref, carry_ref = rest[3 * g + 1:]
    s_idx = pl.program_id(1)
    rows = q_ref.shape[1]
    page = k_refs[0].shape[2]
    keys = page * N_HEADS

    @pl.when(s_idx == 0)
    def _():
        m_ref[...] = jnp.full_like(m_ref, NEG_BIG)
        l_ref[...] = jnp.zeros_like(l_ref)
        acc_ref[...] = jnp.zeros_like(acc_ref)
        carry_ref[...] = jnp.zeros_like(carry_ref)

    qb = q_ref[0].astype(BF16)
    row_head = lax.broadcasted_iota(jnp.int32, (rows, keys), 0) % N_HEADS
    key_head = lax.broadcasted_iota(jnp.int32, (rows, keys), 1) % N_HEADS
    head_match = row_head == key_head

    def online_update(scores, values):
        m_old = m_ref[...]
        m_new = m_old
        for s in scores:
            m_new = jnp.maximum(m_new, jnp.max(s, axis=1, keepdims=True))
        alpha = jnp.exp2(m_old - m_new)
        l_new = alpha * l_ref[...]
        acc = alpha * acc_ref[...]
        for s, v_bf in zip(scores, values):
            p = jnp.exp2(s - m_new)
            l_new = l_new + jnp.sum(p, axis=1, keepdims=True)
            acc = acc + jnp.dot(p.astype(BF16), v_bf, preferred_element_type=F32)
        l_ref[...] = l_new
        acc_ref[...] = acc
        m_ref[...] = m_new

    carry = carry_ref[...]
    scores = []
    for i in range(g):
        kb = k_refs[i][0, 0].reshape(keys, HEAD_DIM).astype(BF16)
        bias = (d_refs[i][0, 0, 0] + carry) * LOG2E
        carry = carry + d_refs[i][0, 0, 1]
        s = lax.dot_general(qb, kb, (((1,), (1,)), ((), ())), preferred_element_type=F32)
        scores.append(jnp.where(head_match, s + bias, NEG_BIG))
    carry_ref[...] = carry
    values = [v_refs[i][0, 0].reshape(keys, HEAD_DIM).astype(BF16) for i in range(g)]
    online_update(scores, values)

    @pl.when(s_idx == pl.num_programs(1) - 1)
    def _():
        kn = kn_ref[0].astype(BF16)
        vn = vn_ref[0].astype(BF16)
        s = lax.dot_general(qb, kn, (((1,), (1,)), ((), ())), preferred_element_type=F32)
        r = lax.broadcasted_iota(jnp.int32, (rows, rows), 0)
        c = lax.broadcasted_iota(jnp.int32, (rows, rows), 1)
        same_head = (r % N_HEADS) == (c % N_HEADS)
        upto = jnp.where(same_head & (r <= c), lfc_ref[0], 0.0)
        cn_row = jnp.sum(upto, axis=0, keepdims=True)
        s = jnp.where(same_head & (c <= r), s - cn_row * LOG2E, NEG_BIG)
        online_update([s], [vn])
        o_ref[0] = acc_ref[...] / l_ref[...]


def _sample_attn(q, k_new, v_new, lf_new, cache_k, cache_v, decay, page_table, layer):
    n, rows, _ = q.shape
    n_pages = page_table.shape[1]
    page = cache_k.shape[2]
    keys = page * N_HEADS
    g = PAGES_PER_STEP
    steps = n_pages // g

    def page_map5(i):
        return lambda b, s, pt: (layer, pt[b, n_pages - 1 - (s * g + i)], 0, 0, 0)

    per_seq = lambda b, s, pt: (b, 0, 0)
    in_specs = [
        pl.BlockSpec((1, rows, HEAD_DIM), per_seq),
        pl.BlockSpec((1, rows, HEAD_DIM), per_seq),
        pl.BlockSpec((1, rows, HEAD_DIM), per_seq),
        pl.BlockSpec((1, rows, 1), per_seq),
    ]
    in_specs += [pl.BlockSpec((1, 1, page, N_HEADS, HEAD_DIM), page_map5(i)) for i in range(g)]
    in_specs += [pl.BlockSpec((1, 1, page, N_HEADS, HEAD_DIM), page_map5(i)) for i in range(g)]
    in_specs += [pl.BlockSpec((1, 1, 2, 1, keys), page_map5(i)) for i in range(g)]
    grid_spec = pltpu.PrefetchScalarGridSpec(
        num_scalar_prefetch=1,
        grid=(n, steps),
        in_specs=in_specs,
        out_specs=pl.BlockSpec((1, rows, HEAD_DIM), per_seq),
        scratch_shapes=[pltpu.VMEM((rows, 1), F32), pltpu.VMEM((rows, 1), F32),
                        pltpu.VMEM((rows, HEAD_DIM), F32), pltpu.VMEM((1, keys), F32)],
    )
    return pl.pallas_call(
        _sample_attn_kernel,
        grid_spec=grid_spec,
        out_shape=jax.ShapeDtypeStruct((n, rows, HEAD_DIM), F32),
        compiler_params=_cparams(("parallel", "arbitrary")),
        name="sample_attn",
    )(page_table, q, k_new, v_new, lf_new.reshape(n, rows, 1),
      *([cache_k] * g), *([cache_v] * g), *([decay] * g))


def _discretise(a_re, a_im, log_dt):
    dt = jnp.exp(log_dt)
    mag = jnp.exp(dt * a_re)
    ab_re = mag * jnp.cos(dt * a_im)
    ab_im = mag * jnp.sin(dt * a_im)
    den = a_re * a_re + a_im * a_im
    f_re = ((ab_re - 1.0) * a_re + ab_im * a_im) / den
    f_im = (ab_im * a_re - (ab_re - 1.0) * a_im) / den
    return ab_re, ab_im, f_re, f_im


def _s5_prep_kernel(pe_ref, b_ref, pf_ref, c_ref, ps_ref,
                    e_ref, f_ref, m_ref, pw_ref, e32_ref, f0_ref):
    t_len = S5_CHUNK
    sc = STATE_COLS
    ab_re, ab_im, f_re, f_im = _discretise(pe_ref[0, 0], pe_ref[0, 1], pe_ref[0, 2])
    b_re = b_ref[0, 0]
    b_im = b_ref[0, 1]
    row_g = lax.broadcasted_iota(jnp.int32, (LANES, sc), 0) // SSM_GROUP
    col_g = lax.broadcasted_iota(jnp.int32, (LANES, sc), 1) // SSM_STATE
    same = row_g == col_g
    w_re = jnp.where(same, f_re * b_re - f_im * b_im, 0.0)
    w_im = jnp.where(same, f_re * b_im + f_im * b_re, 0.0)
    for j in range(t_len):
        s = t_len - 1 - j
        e32_ref[s * LANES:(s + 1) * LANES, 0:sc] = w_re
        e32_ref[s * LANES:(s + 1) * LANES, sc:2 * sc] = w_im
        w_re, w_im = ab_re * w_re - ab_im * w_im, ab_re * w_im + ab_im * w_re
    e_ref[0] = e32_ref[...].astype(BF16)

    ab_re, ab_im, _, _ = _discretise(pf_ref[0, 0], pf_ref[0, 1], pf_ref[0, 2])
    row_g = lax.broadcasted_iota(jnp.int32, (sc, LANES), 0) // SSM_STATE
    col_g = lax.broadcasted_iota(jnp.int32, (sc, LANES), 1) // SSM_GROUP
    same = row_g == col_g
    g_re = jnp.where(same, c_ref[0, 0], 0.0)
    g_im = jnp.where(same, c_ref[0, 1], 0.0)
    f0_ref[0:sc, :] = g_re
    f0_ref[sc:2 * sc, :] = -g_im
    for t in range(t_len):
        g_re, g_im = g_re * ab_re - g_im * ab_im, g_re * ab_im + g_im * ab_re
        f_ref[0, 0:sc, t * LANES:(t + 1) * LANES] = g_re.astype(BF16)
        f_ref[0, sc:2 * sc, t * LANES:(t + 1) * LANES] = (-g_im).astype(BF16)

    lag = jnp.dot(e32_ref[...], f0_ref[...], precision=lax.Precision.HIGHEST,
                  preferred_element_type=F32).astype(BF16)
    m_ref[0] = jnp.zeros(m_ref.shape[1:], BF16)
    for t in range(t_len):
        m_ref[0, 0:(t + 1) * LANES, t * LANES:(t + 1) * LANES] = lag[(t_len - 1 - t) * LANES:, :]

    a_re, a_im, _, _ = _discretise(ps_ref[0, 0:1], ps_ref[0, 1:2], ps_ref[0, 2:3])
    a2_re, a2_im = a_re * a_re - a_im * a_im, 2.0 * a_re * a_im
    a4_re, a4_im = a2_re * a2_re - a2_im * a2_im, 2.0 * a2_re * a2_im
    a8_re, a8_im = a4_re * a4_re - a4_im * a4_im, 2.0 * a4_re * a4_im
    a16_re, a16_im = a8_re * a8_re - a8_im * a8_im, 2.0 * a8_re * a8_im
    pw_ref[0, 0:1, 0:sc] = a4_re
    pw_ref[0, 0:1, sc:2 * sc] = a4_im
    pw_ref[0, 1:2, 0:sc] = a16_re
    pw_ref[0, 1:2, sc:2 * sc] = a16_im


def _s5_prep(a_re, a_im, log_dt, b_re, b_im, c_re, c_im):
    nb, gb, st, ch = N_LANE_BLOCKS, GROUPS_PER_BLOCK, SSM_STATE, SSM_GROUP
    ldt = jnp.broadcast_to(log_dt[:, None], (SSM_GROUPS, st))
    par = jnp.stack([a_re, a_im, ldt], axis=0).reshape(3, nb, gb, st)
    pe = jnp.broadcast_to(par[:, :, :, None, None, :], (3, nb, gb, ch, gb, st))
    pe = pe.reshape(3, nb, LANES, STATE_COLS).transpose(1, 0, 2, 3)
    bb = jnp.stack([b_re, b_im], axis=0).reshape(2, nb, gb, st, ch).transpose(0, 1, 2, 4, 3)
    bb = jnp.broadcast_to(bb[:, :, :, :, None, :], (2, nb, gb, ch, gb, st))
    bb = bb.reshape(2, nb, LANES, STATE_COLS).transpose(1, 0, 2, 3)
    pf = jnp.broadcast_to(par[:, :, :, :, None, None], (3, nb, gb, st, gb, ch))
    pf = pf.reshape(3, nb, STATE_COLS, LANES).transpose(1, 0, 2, 3)
    cc = jnp.stack([c_re, c_im], axis=0).reshape(2, nb, gb, ch, st).transpose(0, 1, 2, 4, 3)
    cc = jnp.broadcast_to(cc[:, :, :, :, None, :], (2, nb, gb, st, gb, ch))
    cc = cc.reshape(2, nb, STATE_COLS, LANES).transpose(1, 0, 2, 3)
    ps = par.reshape(3, nb, STATE_COLS).transpose(1, 0, 2)

    tl = S5_CHUNK * LANES
    blk4 = lambda i: (i, 0, 0, 0)
    blk3 = lambda i: (i, 0, 0)
    return pl.pallas_call(
        _s5_prep_kernel,
        grid=(nb,),
        in_specs=[
            pl.BlockSpec((1, 3, LANES, STATE_COLS), blk4),
            pl.BlockSpec((1, 2, LANES, STATE_COLS), blk4),
            pl.BlockSpec((1, 3, STATE_COLS, LANES), blk4),
            pl.BlockSpec((1, 2, STATE_COLS, LANES), blk4),
            pl.BlockSpec((1, 3, STATE_COLS), blk3),
        ],
        out_specs=(
            pl.BlockSpec((1, tl, 2 * STATE_COLS), blk3),
            pl.BlockSpec((1, 2 * STATE_COLS, tl), blk3),
            pl.BlockSpec((1, tl, tl), blk3),
            pl.BlockSpec((1, 2, 2 * STATE_COLS), blk3),
        ),
        out_shape=(
            jax.ShapeDtypeStruct((nb, tl, 2 * STATE_COLS), BF16),
            jax.ShapeDtypeStruct((nb, 2 * STATE_COLS, tl), BF16),
            jax.ShapeDtypeStruct((nb, tl, tl), BF16),
            jax.ShapeDtypeStruct((nb, 2, 2 * STATE_COLS), F32),
        ),
        scratch_shapes=[pltpu.VMEM((tl, 2 * STATE_COLS), F32),
                        pltpu.VMEM((2 * STATE_COLS, LANES), F32)],
        compiler_params=_cparams(("parallel",)),
        name="s5_prep",
    )(pe, bb, pf, cc, ps)


def _s5_prompt_kernel(u_ref, e_ref, m_ref, f_ref, pw_ref, d_ref,
                      y_ref, hl_ref, ub_ref, sloc_ref, hp_ref, hpb_ref):
    ct = pl.program_id(1)
    n_rows = u_ref.shape[1]
    n_batch = hl_ref.shape[1]
    per_batch = n_rows // n_batch
    sc = STATE_COLS

    @pl.when(ct == 0)
    def _():
        ub = u_ref[0].astype(BF16)
        ub_ref[...] = ub
        sloc_ref[...] = jnp.dot(ub, e_ref[0], preferred_element_type=F32)
        p_re = pw_ref[0, 1:2, 0:sc]
        p_im = pw_ref[0, 1:2, sc:2 * sc]

        def body(k, carry):
            new = []
            for b in range(n_batch):
                h_re, h_im = carry[2 * b], carry[2 * b + 1]
                row = b * per_batch + k
                hp_ref[pl.ds(row, 1), 0:sc] = h_re
                hp_ref[pl.ds(row, 1), sc:2 * sc] = h_im
                s_re = sloc_ref[pl.ds(row, 1), 0:sc]
                s_im = sloc_ref[pl.ds(row, 1), sc:2 * sc]
                new.append(p_re * h_re - p_im * h_im + s_re)
                new.append(p_re * h_im + p_im * h_re + s_im)
            return tuple(new)

        zero = jnp.zeros((1, sc), F32)
        final = lax.fori_loop(0, per_batch, body, (zero,) * (2 * n_batch))
        for b in range(n_batch):
            hl_ref[0, b:b + 1, 0:sc] = final[2 * b]
            hl_ref[0, b:b + 1, sc:2 * sc] = final[2 * b + 1]
        hpb_ref[...] = hp_ref[...].astype(BF16)

    w = y_ref.shape[2]
    col = pl.multiple_of(ct * w, w)
    y = jnp.dot(ub_ref[...], m_ref[0], preferred_element_type=F32)
    y = y + jnp.dot(hpb_ref[...], f_ref[0], preferred_element_type=F32)
    y_ref[0] = y + d_ref[0] * u_ref[0, :, pl.ds(col, w)]


def _s5_prompt(u_chunks, e_op, m_op, f_op, pw, d_tiled, n_batch):
    nb, n_rows, tl = u_chunks.shape
    w = 512
    return pl.pallas_call(
        _s5_prompt_kernel,
        grid=(nb, tl // w),
        in_specs=[
            pl.BlockSpec((1, n_rows, tl), lambda i, c: (i, 0, 0)),
            pl.BlockSpec((1, tl, 2 * STATE_COLS), lambda i, c: (i, 0, 0)),
            pl.BlockSpec((1, tl, w), lambda i, c: (i, 0, c)),
            pl.BlockSpec((1, 2 * STATE_COLS, w), lambda i, c: (i, 0, c)),
            pl.BlockSpec((1, 2, 2 * STATE_COLS), lambda i, c: (i, 0, 0)),
            pl.BlockSpec((1, 1, w), lambda i, c: (i, 0, 0)),
        ],
        out_specs=(
            pl.BlockSpec((1, n_rows, w), lambda i, c: (i, 0, c)),
            pl.BlockSpec((1, n_batch, 2 * STATE_COLS), lambda i, c: (i, 0, 0)),
        ),
        out_shape=(
            jax.ShapeDtypeStruct((nb, n_rows, tl), F32),
            jax.ShapeDtypeStruct((nb, n_batch, 2 * STATE_COLS), F32),
        ),
        scratch_shapes=[pltpu.VMEM((n_rows, tl), BF16),
                        pltpu.VMEM((n_rows, 2 * STATE_COLS), F32),
                        pltpu.VMEM((n_rows, 2 * STATE_COLS), F32),
                        pltpu.VMEM((n_rows, 2 * STATE_COLS), BF16)],
        compiler_params=_cparams(("parallel", "arbitrary")),
        name="s5_prompt",
    )(u_chunks, e_op, m_op, f_op, pw, d_tiled)


def _s5_sample_kernel(u_ref, e_ref, m_ref, f_ref, pw_ref, d_ref, h0_ref, y_ref, hl_ref):
    sc = STATE_COLS
    u = u_ref[0]
    ub = u.astype(BF16)
    h0 = h0_ref[0]
    y = jnp.dot(ub, m_ref[0], preferred_element_type=F32)
    y = y + jnp.dot(h0.astype(BF16), f_ref[0], preferred_element_type=F32)
    y_ref[0] = y + d_ref[0] * u
    sloc = jnp.dot(ub, e_ref[0], preferred_element_type=F32)
    p_re = pw_ref[0, 0:1, 0:sc]
    p_im = pw_ref[0, 0:1, sc:2 * sc]
    h_re = h0[:, 0:sc]
    h_im = h0[:, sc:2 * sc]
    hl_ref[0, :, 0:sc] = p_re * h_re - p_im * h_im + sloc[:, 0:sc]
    hl_ref[0, :, sc:2 * sc] = p_re * h_im + p_im * h_re + sloc[:, sc:2 * sc]


def _s5_sample(u_chunks, e_op, m_op, f_op, pw, d_tiled, h0):
    nb, n_seq, w = u_chunks.shape
    tl = e_op.shape[1]
    e_blk = (tl - w) // w
    return pl.pallas_call(
        _s5_sample_kernel,
        grid=(nb,),
        in_specs=[
            pl.BlockSpec((1, n_seq, w), lambda i: (i, 0, 0)),
            pl.BlockSpec((1, w, 2 * STATE_COLS), lambda i: (i, e_blk, 0)),
            pl.BlockSpec((1, w, w), lambda i: (i, 0, 0)),
            pl.BlockSpec((1, 2 * STATE_COLS, w), lambda i: (i, 0, 0)),
            pl.BlockSpec((1, 2, 2 * STATE_COLS), lambda i: (i, 0, 0)),
            pl.BlockSpec((1, 1, w), lambda i: (i, 0, 0)),
            pl.BlockSpec((1, n_seq, 2 * STATE_COLS), lambda i: (i, 0, 0)),
        ],
        out_specs=(
            pl.BlockSpec((1, n_seq, w), lambda i: (i, 0, 0)),
            pl.BlockSpec((1, n_seq, 2 * STATE_COLS), lambda i: (i, 0, 0)),
        ),
        out_shape=(
            jax.ShapeDtypeStruct((nb, n_seq, w), F32),
            jax.ShapeDtypeStruct((nb, n_seq, 2 * STATE_COLS), F32),
        ),
        compiler_params=_cparams(("parallel",)),
        name="s5_sample",
    )(u_chunks, e_op, m_op, f_op, pw, d_tiled, h0)


def _mix_out_kernel(x_ref, attn_ref, y_ref, gw_ref, gb_ref, ga_ref, gs_ref, wo_ref, gp_ref,
                    o_ref, ycat_ref, mg_ref):
    for lb in range(N_LANE_BLOCKS):
        ycat_ref[:, lb * LANES:(lb + 1) * LANES] = jax.nn.gelu(y_ref[lb])
    y = ycat_ref[...]
    z = jnp.dot(y.astype(BF16), gw_ref[...], preferred_element_type=F32) + gb_ref[...]
    s = y * jax.nn.sigmoid(z)
    mg_ref[:, 0:D_ATTN] = _rms(attn_ref[...], ga_ref[...]).astype(BF16)
    mg_ref[:, D_ATTN:] = _rms(s, gs_ref[...]).astype(BF16)
    m = jnp.dot(mg_ref[...], wo_ref[...], preferred_element_type=F32)
    o_ref[...] = x_ref[...] + _rms(m, gp_ref[...])


def _mix_out(x, attn, y_blk, glu_w, glu_b, g_attn, g_ssm, w_out, g_post, layer):
    m = x.shape[0]
    tm = 256
    row = lambda i: (i, 0)
    const = lambda i: (0, 0)
    per_layer = lambda i: (layer, 0, 0)
    return pl.pallas_call(
        _mix_out_kernel,
        grid=(m // tm,),
        in_specs=[
            pl.BlockSpec((tm, D_MODEL), row),
            pl.BlockSpec((tm, D_ATTN), row),
            pl.BlockSpec((N_LANE_BLOCKS, tm, LANES), lambda i: (0, i, 0)),
            pl.BlockSpec((None, D_SSM, D_SSM), per_layer),
            pl.BlockSpec((1, D_SSM), const),
            pl.BlockSpec((1, D_ATTN), const),
            pl.BlockSpec((1, D_SSM), const),
            pl.BlockSpec((None, D_MODEL, D_MODEL), per_layer),
            pl.BlockSpec((1, D_MODEL), const),
        ],
        out_specs=pl.BlockSpec((tm, D_MODEL), row),
        out_shape=jax.ShapeDtypeStruct((m, D_MODEL), F32),
        scratch_shapes=[pltpu.VMEM((tm, D_SSM), F32), pltpu.VMEM((tm, D_MODEL), BF16)],
        compiler_params=_cparams(("parallel",)),
        name="mix_out",
    )(x, attn, y_blk, glu_w, glu_b, g_attn, g_ssm, w_out, g_post)


def _state_to_blocks(h):
    n = h.shape[0]
    h = h.reshape(n, N_LANE_BLOCKS, STATE_COLS, 2)
    return h.transpose(1, 0, 3, 2).reshape(N_LANE_BLOCKS, n, 2 * STATE_COLS)


def _blocks_to_state(h):
    n = h.shape[1]
    h = h.reshape(N_LANE_BLOCKS, n, 2, STATE_COLS).transpose(1, 0, 3, 2)
    return h.reshape(n, SSM_GROUPS, SSM_STATE, 2)


def _trunk_layer(x, l, w, mixer):
    ng = w["ng"][l]
    x = _ffn_half(x, ng[0], ng[1], w["wg"], w["wu"], w["wd"], l, 0)
    u_blk, q, k, v, kb, vb, lf = _mix_in(x, ng[2], w["w_main"], w["w_f"], w["b_f"], l)
    logf = lf[:, :N_HEADS]
    attn, y_blk, h_last = mixer(u_blk, q, k, v, kb, vb, logf)
    x = _mix_out(x, attn, y_blk, w["glu_w"], w["glu_b"][l], w["g_attn"][l], w["g_ssm"][l],
                 w["w_out"], ng[3], l)
    x = _ffn_half(x, ng[4], ng[5], w["wg"], w["wu"], w["wd"], l, 1)
    return x, k, v, logf, h_last


def kernel(x_prompt, x_sample, cache_k, cache_v, cache_logf, state_ssm, page_table, norm_g, ffn_gate, ffn_up, ffn_down, w_in, b_forget, ssm_a_re, ssm_a_im, ssm_log_dt, ssm_b_re, ssm_b_im, ssm_c_re, ssm_c_im, ssm_d, glu_w, glu_b, g_attn_out, g_ssm_out, w_out):
    batch, seq, _ = x_prompt.shape
    n_seq, n_new, _ = x_sample.shape
    depth = norm_g.shape[0]
    n_pool, page = cache_k.shape[1], cache_k.shape[2]
    assert seq % ROW_TILE == 0 and seq % ATT_TILE == 0 and seq % S5_CHUNK == 0
    assert (n_seq * n_new) % 256 == 0 and page_table.shape[1] % PAGES_PER_STEP == 0
    assert n_pool % DECAY_PAGES == 0 and N_HEADS == SUBLANES

    xp = x_prompt.reshape(batch * seq, D_MODEL)
    xs = x_sample.reshape(n_seq * n_new, D_MODEL)

    ff_pad = D_FF_PAD - D_FF
    n_main = D_SSM + 3 * D_ATTN
    w = {
        "ng": [[norm_g[l, i].reshape(1, D_MODEL) for i in range(6)] for l in range(depth)],
        "wg": jnp.concatenate(
            [ffn_gate.astype(BF16), jnp.zeros(ffn_gate.shape[:3] + (ff_pad,), BF16)], axis=3),
        "wu": jnp.concatenate(
            [ffn_up.astype(BF16), jnp.zeros(ffn_up.shape[:3] + (ff_pad,), BF16)], axis=3),
        "wd": jnp.concatenate(
            [ffn_down.astype(BF16), jnp.zeros(ffn_down.shape[:2] + (ff_pad, D_MODEL), BF16)],
            axis=2),
        "w_main": w_in[:, :, :n_main].astype(BF16),
        "w_f": jnp.pad(w_in[:, :, n_main:], ((0, 0), (0, 0), (0, LANES - N_HEADS))).astype(BF16),
        "b_f": jnp.pad(b_forget, ((0, 0), (0, LANES - N_HEADS))).reshape(depth, 1, LANES),
        "glu_w": glu_w.astype(BF16),
        "glu_b": [glu_b[l].reshape(1, D_SSM) for l in range(depth)],
        "g_attn": [g_attn_out[l].reshape(1, D_ATTN) for l in range(depth)],
        "g_ssm": [g_ssm_out[l].reshape(1, D_SSM) for l in range(depth)],
        "w_out": w_out.astype(BF16),
    }

    decay = _page_decay(cache_logf.transpose(0, 1, 3, 2))
    decay = decay.transpose(0, 2, 1, 4, 3).reshape(depth, n_pool, 2, 1, page * N_HEADS)

    outs = {name: [] for name in ("kp", "vp", "lfp", "sp", "ks", "vs", "lfs", "ss")}
    for l in range(depth):
        e_op, f_op, m_op, pw = _s5_prep(ssm_a_re[l], ssm_a_im[l], ssm_log_dt[l],
                                        ssm_b_re[l], ssm_b_im[l], ssm_c_re[l], ssm_c_im[l])
        d_blk = ssm_d[l].reshape(N_LANE_BLOCKS, 1, LANES)
        d_prompt = jnp.tile(d_blk, (1, 1, 512 // LANES))
        d_sample = jnp.tile(d_blk, (1, 1, n_new))

        def prompt_mixer(u_blk, q, k, v, kb, vb, logf):
            lf_t = logf.reshape(batch, seq, N_HEADS).transpose(0, 2, 1)
            c_col = _cumsum_heads(lf_t).transpose(0, 2, 1)
            attn = _flash_prompt(q.reshape(batch, seq, D_ATTN), kb.reshape(batch, seq, D_ATTN),
                                 vb.reshape(batch, seq, D_ATTN), c_col)
            u_chunks = u_blk.reshape(N_LANE_BLOCKS, batch * seq // S5_CHUNK, S5_CHUNK * LANES)
            y, h_last = _s5_prompt(u_chunks, e_op, m_op, f_op, pw, d_prompt, batch)
            return (attn.reshape(batch * seq, D_ATTN),
                    y.reshape(N_LANE_BLOCKS, batch * seq, LANES), _blocks_to_state(h_last))

        def sample_mixer(u_blk, q, k, v, kb, vb, logf):
            rows = n_new * N_HEADS
            attn = _sample_attn(q.astype(F32).reshape(n_seq, rows, HEAD_DIM),
                                k.reshape(n_seq, rows, HEAD_DIM), v.reshape(n_seq, rows, HEAD_DIM),
                                logf.reshape(n_seq, rows), cache_k, cache_v, decay,
                                page_table, l)
            u_chunks = u_blk.reshape(N_LANE_BLOCKS, n_seq, n_new * LANES)
            y, h_last = _s5_sample(u_chunks, e_op, m_op, f_op, pw, d_sample,
                                   _state_to_blocks(state_ssm[l]))
            return (attn.reshape(n_seq * n_new, D_ATTN),
                    y.reshape(N_LANE_BLOCKS, n_seq * n_new, LANES), _blocks_to_state(h_last))

        xp, k1, v1, lf1, s1 = _trunk_layer(xp, l, w, prompt_mixer)
        xs, k2, v2, lf2, s2 = _trunk_layer(xs, l, w, sample_mixer)
        outs["kp"].append(k1.reshape(batch, seq, N_HEADS, HEAD_DIM))
        outs["vp"].append(v1.reshape(batch, seq, N_HEADS, HEAD_DIM))
        outs["lfp"].append(lf1.reshape(batch, seq, N_HEADS))
        outs["sp"].append(s1)
        outs["ks"].append(k2.reshape(n_seq, n_new, N_HEADS, HEAD_DIM))
        outs["vs"].append(v2.reshape(n_seq, n_new, N_HEADS, HEAD_DIM))
        outs["lfs"].append(lf2.reshape(n_seq, n_new, N_HEADS))
        outs["ss"].append(s2)

    return (xp.reshape(batch, seq, D_MODEL), xs.reshape(n_seq, n_new, D_MODEL),
            jnp.stack(outs["kp"]), jnp.stack(outs["vp"]), jnp.stack(outs["lfp"]),
            jnp.stack(outs["sp"]),
            jnp.stack(outs["ks"]), jnp.stack(outs["vs"]), jnp.stack(outs["lfs"]),
            jnp.stack(outs["ss"]))
```

```python
import functools
import math

import jax
import jax.numpy as jnp
from jax import lax
from jax.experimental import pallas as pl
from jax.experimental.pallas import tpu as pltpu

F32 = jnp.float32
BF16 = jnp.bfloat16

D_MODEL = 2048
D_SSM = 1024
D_ATTN = 1024
HEAD_DIM = 128
N_HEADS = 8
SSM_GROUP = 16
SSM_GROUPS = 64
SSM_STATE = 64
D_FF = 5504
EPS = 1e-6
LOG2E = math.log2(math.e)

LANES = 128
SUBLANES = 8
GROUPS_PER_BLOCK = LANES // SSM_GROUP
N_LANE_BLOCKS = D_SSM // LANES
STATE_COLS = GROUPS_PER_BLOCK * SSM_STATE
S5_CHUNK = 16
FF_TILE = 512
FF_STEPS = (D_FF + FF_TILE - 1) // FF_TILE
FF_TAIL = D_FF - (FF_STEPS - 1) * FF_TILE
ROW_TILE = 512
ATT_TILE = 512
ATT_COLS = 512
HEADS_PER_STEP = 2
PAGES_PER_STEP = 8
DECAY_PAGES = 256
VMEM_LIMIT = 56 * 1024 * 1024
VMEM_LIMIT_FUSED = 60 * 1024 * 1024
NEG_BIG = -1e30


def _cparams(sem):
    return pltpu.CompilerParams(dimension_semantics=sem, vmem_limit_bytes=VMEM_LIMIT)


def _rms(x, g):
    ms = jnp.mean(x * x, axis=-1, keepdims=True)
    return x * lax.rsqrt(ms + EPS) * g


def _log_sigmoid(z):
    return jnp.minimum(z, 0.0) - jnp.log1p(jnp.exp(-jnp.abs(z)))


def _ffn_step(x_ref, gpre_ref, gpost_ref, wg_ref, wu_ref, wd_ref, o_ref, h_ref, acc_ref):
    j = pl.program_id(1)
    last = pl.num_programs(1) - 1

    @pl.when(j == 0)
    def _():
        h_ref[...] = _rms(x_ref[...], gpre_ref[...]).astype(BF16)
        acc_ref[...] = jnp.zeros_like(acc_ref)

    def hidden_tile(width):
        h = h_ref[...]
        g = jnp.dot(h, wg_ref[:, 0:width], preferred_element_type=F32)
        u = jnp.dot(h, wu_ref[:, 0:width], preferred_element_type=F32)
        a = (g * jax.nn.sigmoid(g) * u).astype(BF16)
        acc_ref[...] += jnp.dot(a, wd_ref[0:width, :], preferred_element_type=F32)

    if FF_TAIL == FF_TILE:
        hidden_tile(FF_TILE)
    else:
        pl.when(j < last)(lambda: hidden_tile(FF_TILE))
        pl.when(j == last)(lambda: hidden_tile(FF_TAIL))

    @pl.when(j == last)
    def _():
        o_ref[...] = x_ref[...] + 0.5 * _rms(acc_ref[...], gpost_ref[...])


def _ffn_in_specs(tm, layer, half, index):
    def at(f):
        return lambda *a: f(*index(*a))
    return [
        pl.BlockSpec((tm, D_MODEL), at(lambda i, j: (i, 0))),
        pl.BlockSpec((1, D_MODEL), at(lambda i, j: (0, 0))),
        pl.BlockSpec((1, D_MODEL), at(lambda i, j: (0, 0))),
        pl.BlockSpec((None, None, D_MODEL, FF_TILE), at(lambda i, j: (layer, half, 0, j))),
        pl.BlockSpec((None, None, D_MODEL, FF_TILE), at(lambda i, j: (layer, half, 0, j))),
        pl.BlockSpec((None, None, FF_TILE, D_MODEL), at(lambda i, j: (layer, half, j, 0))),
    ]


def _ffn_half(x, g_pre, g_post, wg, wu, wd, layer, half):
    m = x.shape[0]
    tm = min(ROW_TILE, m)
    return pl.pallas_call(
        _ffn_step,
        grid=(m // tm, FF_STEPS),
        in_specs=_ffn_in_specs(tm, layer, half, lambda i, j: (i, j)),
        out_specs=pl.BlockSpec((tm, D_MODEL), lambda i, j: (i, 0)),
        out_shape=jax.ShapeDtypeStruct((m, D_MODEL), F32),
        scratch_shapes=[pltpu.VMEM((tm, D_MODEL), BF16), pltpu.VMEM((tm, D_MODEL), F32)],
        compiler_params=_cparams(("parallel", "arbitrary")),
        name="ffn_half",
    )(x, g_pre, g_post, wg, wu, wd)


def _mix_in_kernel(x_ref, g_ref, w_ref, wf_ref, bf_ref,
                   u_ref, q_ref, k_ref, v_ref, kb_ref, vb_ref, lf_ref, h_ref):
    j = pl.program_id(1)

    @pl.when(j == 0)
    def _():
        h = _rms(x_ref[...], g_ref[...]).astype(BF16)
        h_ref[...] = h
        z = jnp.dot(h, wf_ref[...], preferred_element_type=F32) + bf_ref[...]
        lf_ref[...] = _log_sigmoid(z)

    p = jnp.dot(h_ref[...], w_ref[...], preferred_element_type=F32)

    @pl.when(j == 0)
    def _():
        for lb in range(N_LANE_BLOCKS):
            u_ref[lb] = p[:, lb * LANES:(lb + 1) * LANES]

    @pl.when(j == 1)
    def _():
        q_ref[...] = (p * (HEAD_DIM ** -0.5 * LOG2E)).astype(BF16)

    @pl.when(j == 2)
    def _():
        k_ref[...] = p
        kb_ref[...] = p.astype(BF16)

    @pl.when(j == 3)
    def _():
        v_ref[...] = p
        vb_ref[...] = p.astype(BF16)


def _mix_in(x, g, w_main, w_f, b_f, layer):
    m = x.shape[0]
    tm = min(ROW_TILE, m)
    row = lambda i, j: (i, 0)
    out_shape = (
        jax.ShapeDtypeStruct((N_LANE_BLOCKS, m, LANES), F32),
        jax.ShapeDtypeStruct((m, D_ATTN), BF16),
        jax.ShapeDtypeStruct((m, D_ATTN), F32),
        jax.ShapeDtypeStruct((m, D_ATTN), F32),
        jax.ShapeDtypeStruct((m, D_ATTN), BF16),
        jax.ShapeDtypeStruct((m, D_ATTN), BF16),
        jax.ShapeDtypeStruct((m, LANES), F32),
    )
    out_specs = (
        pl.BlockSpec((N_LANE_BLOCKS, tm, LANES), lambda i, j: (0, i, 0)),
        pl.BlockSpec((tm, D_ATTN), row),
        pl.BlockSpec((tm, D_ATTN), row),
        pl.BlockSpec((tm, D_ATTN), row),
        pl.BlockSpec((tm, D_ATTN), row),
        pl.BlockSpec((tm, D_ATTN), row),
        pl.BlockSpec((tm, LANES), row),
    )
    return pl.pallas_call(
        _mix_in_kernel,
        grid=(m // tm, 4),
        in_specs=[
            pl.BlockSpec((tm, D_MODEL), row),
            pl.BlockSpec((1, D_MODEL), lambda i, j: (0, 0)),
            pl.BlockSpec((None, D_MODEL, D_ATTN), lambda i, j: (layer, 0, j)),
            pl.BlockSpec((None, D_MODEL, LANES), lambda i, j: (layer, 0, 0)),
            pl.BlockSpec((None, 1, LANES), lambda i, j: (layer, 0, 0)),
        ],
        out_specs=out_specs,
        out_shape=out_shape,
        scratch_shapes=[pltpu.VMEM((tm, D_MODEL), BF16)],
        compiler_params=_cparams(("parallel", "arbitrary")),
        name="mix_in",
    )(x, g, w_main, w_f, b_f)


def _cumsum_kernel(x_ref, o_ref):
    blk = 512
    n = x_ref.shape[2]
    r = lax.broadcasted_iota(jnp.int32, (blk, blk), 0)
    c = lax.broadcasted_iota(jnp.int32, (blk, blk), 1)
    tri = (r <= c).astype(F32)
    carry = jnp.zeros((N_HEADS, 1), F32)
    for b in range(n // blk):
        xb = x_ref[0, :, b * blk:(b + 1) * blk]
        cb = jnp.dot(xb, tri, precision=lax.Precision.HIGHEST,
                     preferred_element_type=F32) + carry
        o_ref[0, :, b * blk:(b + 1) * blk] = cb * LOG2E
        carry = cb[:, blk - 1:blk]


def _cumsum_heads(lf_t):
    b, h, n = lf_t.shape
    return pl.pallas_call(
        _cumsum_kernel,
        grid=(b,),
        in_specs=[pl.BlockSpec((1, h, n), lambda i: (i, 0, 0))],
        out_specs=pl.BlockSpec((1, h, n), lambda i: (i, 0, 0)),
        out_shape=jax.ShapeDtypeStruct((b, h, n), F32),
        compiler_params=_cparams(("parallel",)),
        name="logf_cumsum",
    )(lf_t)


def _flash_kernel(q_ref, k_ref, v_ref, ccol_ref, o_ref, cb_ref, m_ref, l_ref, acc_ref):
    hb = pl.program_id(1)
    qi = pl.program_id(2)
    t = ATT_TILE
    n = k_ref.shape[1]

    @pl.when(qi == 0)
    def _():
        cc = ccol_ref[0]
        lane = lax.broadcasted_iota(jnp.int32, cc.shape, 1)
        for a in range(HEADS_PER_STEP):
            col = jnp.sum(jnp.where(lane == hb * HEADS_PER_STEP + a, cc, 0.0),
                          axis=1, keepdims=True)
            cb_ref[a] = jnp.broadcast_to(col, (n, LANES))

    m_ref[...] = jnp.full_like(m_ref, NEG_BIG)
    l_ref[...] = jnp.zeros_like(l_ref)
    acc_ref[...] = jnp.zeros_like(acc_ref)

    def step(ki, masked):
        start = pl.multiple_of(ki * t, t)
        for a in range(HEADS_PER_STEP):
            hl = slice(a * HEAD_DIM, (a + 1) * HEAD_DIM)
            k = k_ref[0, pl.ds(start, t), hl]
            v = v_ref[0, pl.ds(start, t), hl]
            cb = cb_ref[a, pl.ds(start, t), :]
            cb = jnp.concatenate([cb] * (ATT_COLS // LANES), axis=1)
            for j in range(t // ATT_COLS):
                cols = slice(j * ATT_COLS, (j + 1) * ATT_COLS)
                q = q_ref[0, cols, hl]
                s = lax.dot_general(k, q, (((1,), (1,)), ((), ())),
                                    preferred_element_type=F32) - cb
                if masked:
                    r = lax.broadcasted_iota(jnp.int32, s.shape, 0)
                    c = lax.broadcasted_iota(jnp.int32, s.shape, 1) + j * ATT_COLS
                    s = jnp.where(r <= c, s, NEG_BIG)
                m_old = m_ref[a, :, cols]
                m_new = jnp.maximum(m_old, jnp.max(s, axis=0, keepdims=True))
                alpha = jnp.exp2(m_old - m_new)
                p = jnp.exp2(s - m_new)
                l_ref[a, :, cols] = alpha * l_ref[a, :, cols] + jnp.sum(p, axis=0, keepdims=True)
                pv = lax.dot_general(v, p.astype(BF16), (((0,), (0,)), ((), ())),
                                     preferred_element_type=F32)
                acc_ref[a, :, cols] = alpha * acc_ref[a, :, cols] + pv
                m_ref[a, :, cols] = m_new

    def body(ki, carry):
        step(ki, False)
        return carry

    lax.fori_loop(0, qi, body, 0)
    step(qi, True)
    for a in range(HEADS_PER_STEP):
        o_ref[0, :, a * HEAD_DIM:(a + 1) * HEAD_DIM] = (acc_ref[a] / l_ref[a]).T


def _flash_prompt(q, kb, vb, c_col):
    b, n, _ = q.shape
    t = ATT_TILE
    w = HEADS_PER_STEP * HEAD_DIM
    return pl.pallas_call(
        _flash_kernel,
        grid=(b, N_HEADS // HEADS_PER_STEP, n // t),
        in_specs=[
            pl.BlockSpec((1, t, w), lambda bi, h, qi: (bi, qi, h)),
            pl.BlockSpec((1, n, w), lambda bi, h, qi: (bi, 0, h)),
            pl.BlockSpec((1, n, w), lambda bi, h, qi: (bi, 0, h)),
            pl.BlockSpec((1, n, N_HEADS), lambda bi, h, qi: (bi, 0, 0)),
        ],
        out_specs=pl.BlockSpec((1, t, w), lambda bi, h, qi: (bi, qi, h)),
        out_shape=jax.ShapeDtypeStruct((b, n, D_ATTN), F32),
        scratch_shapes=[pltpu.VMEM((HEADS_PER_STEP, n, LANES), F32),
                        pltpu.VMEM((HEADS_PER_STEP, 1, t), F32),
                        pltpu.VMEM((HEADS_PER_STEP, 1, t), F32),
                        pltpu.VMEM((HEADS_PER_STEP, HEAD_DIM, t), F32)],
        compiler_params=_cparams(("arbitrary", "arbitrary", "arbitrary")),
        name="flash_prompt",
    )(q, kb, vb, c_col)


def _page_decay_kernel(lf_ref, o_ref):
    n_pages, heads, page = lf_ref.shape[1:]
    r = lax.broadcasted_iota(jnp.int32, (page, page), 0)
    c = lax.broadcasted_iota(jnp.int32, (page, page), 1)
    later = (r > c).astype(F32)
    x = lf_ref[0].reshape(n_pages * heads, page)
    suffix = jnp.dot(x, later, precision=lax.Precision.HIGHEST, preferred_element_type=F32)
    total = jnp.sum(x, axis=1, keepdims=True)
    o_ref[0, 0] = suffix.reshape(n_pages, heads, page)
    o_ref[0, 1] = jnp.broadcast_to(total, x.shape).reshape(n_pages, heads, page)


def _page_decay(cache_lf_t):
    depth, pool, heads, page = cache_lf_t.shape
    pb = DECAY_PAGES
    return pl.pallas_call(
        _page_decay_kernel,
        grid=(depth, pool // pb),
        in_specs=[pl.BlockSpec((1, pb, heads, page), lambda d, i: (d, i, 0, 0))],
        out_specs=pl.BlockSpec((1, 2, pb, heads, page), lambda d, i: (d, 0, i, 0, 0)),
        out_shape=jax.ShapeDtypeStruct((depth, 2, pool, heads, page), F32),
        compiler_params=_cparams(("parallel", "parallel")),
        name="page_decay",
    )(cache_lf_t)


def _attn_step(first, last, q_ref, kn_ref, vn_ref, lfc_ref, k_refs, v_refs, d_refs, o_ref,
               m_ref, l_ref, acc_ref, carry_ref):
    g = len(k_refs)
    rows = q_ref.shape[1]
    page = k_refs[0].shape[2]
    keys = page * N_HEADS

    @pl.when(first)
    def _():
        m_ref[...] = jnp.full_like(m_ref, NEG_BIG)
        l_ref[...] = jnp.zeros_like(l_ref)
        acc_ref[...] = jnp.zeros_like(acc_ref)
        carry_ref[...] = jnp.zeros_like(carry_ref)

    qb = q_ref[0].astype(BF16)
    row_head = lax.broadcasted_iota(jnp.int32, (rows, keys), 0) % N_HEADS
    key_head = lax.broadcasted_iota(jnp.int32, (rows, keys), 1) % N_HEADS
    head_match = row_head == key_head

    def online_update(scores, values):
        m_old = m_ref[...]
        m_new = m_old
        for s in scores:
            m_new = jnp.maximum(m_new, jnp.max(s, axis=1, keepdims=True))
        alpha = jnp.exp2(m_old - m_new)
        l_new = alpha * l_ref[...]
        acc = alpha * acc_ref[...]
        for s, v_bf in zip(scores, values):
            p = jnp.exp2(s - m_new)
            l_new = l_new + jnp.sum(p, axis=1, keepdims=True)
            acc = acc + jnp.dot(p.astype(BF16), v_bf, preferred_element_type=F32)
        l_ref[...] = l_new
        acc_ref[...] = acc
        m_ref[...] = m_new

    carry = carry_ref[...]
    scores = []
    for i in range(g):
        kb = k_refs[i][0, 0].reshape(keys, HEAD_DIM).astype(BF16)
        bias = (d_refs[i][0, 0, 0] + carry) * LOG2E
        carry = carry + d_refs[i][0, 0, 1]
        s = lax.dot_general(qb, kb, (((1,), (1,)), ((), ())), preferred_element_type=F32)
        scores.append(jnp.where(head_match, s + bias, NEG_BIG))
    carry_ref[...] = carry
    values = [v_refs[i][0, 0].reshape(keys, HEAD_DIM).astype(BF16) for i in range(g)]
    online_update(scores, values)

    @pl.when(last)
    def _():
        kn = kn_ref[0].astype(BF16)
        vn = vn_ref[0].astype(BF16)
        s = lax.dot_general(qb, kn, (((1,), (1,)), ((), ())), preferred_element_type=F32)
        r = lax.broadcasted_iota(jnp.int32, (rows, rows), 0)
        c = lax.broadcasted_iota(jnp.int32, (rows, rows), 1)
        same_head = (r % N_HEADS) == (c % N_HEADS)
        upto = jnp.where(same_head & (r <= c), lfc_ref[0], 0.0)
        cn_row = jnp.sum(upto, axis=0, keepdims=True)
        s = jnp.where(same_head & (c <= r), s - cn_row * LOG2E, NEG_BIG)
        online_update([s], [vn])
        o_ref[0] = acc_ref[...] / l_ref[...]


def _ffn_attn_kernel(n_tasks, steps_per_seq, pt_ref, x_ref, gpre_ref, gpost_ref, wg_ref, wu_ref,
                     wd_ref, q_ref, kn_ref, vn_ref, lfc_ref, *rest):
    g = PAGES_PER_STEP
    k_refs = rest[0:g]
    v_refs = rest[g:2 * g]
    d_refs = rest[2 * g:3 * g]
    o_ref, oa_ref = rest[3 * g:3 * g + 2]
    h_ref, acc_ref, m_ref, l_ref, aacc_ref, carry_ref = rest[3 * g + 2:]
    _ffn_step(x_ref, gpre_ref, gpost_ref, wg_ref, wu_ref, wd_ref, o_ref, h_ref, acc_ref)
    task = pl.program_id(0) * pl.num_programs(1) + pl.program_id(1)
    part = task % steps_per_seq

    @pl.when(task < n_tasks)
    def _():
        _attn_step(part == 0, part == steps_per_seq - 1, q_ref, kn_ref, vn_ref, lfc_ref,
                   k_refs, v_refs, d_refs, oa_ref, m_ref, l_ref, aacc_ref, carry_ref)


def _ffn_half_with_attn(x, g_pre, g_post, wg, wu, wd, layer, half,
                        q, k_new, v_new, lf_new, cache_k, cache_v, decay, page_table, seq0):
    m = x.shape[0]
    tm = min(ROW_TILE, m)
    n, rows, _ = q.shape
    n_pages = page_table.shape[1]
    page = cache_k.shape[2]
    keys = page * N_HEADS
    g = PAGES_PER_STEP
    steps_per_seq = n_pages // g
    n_tasks = n * steps_per_seq
    grid = (m // tm, FF_STEPS)
    assert n_tasks <= grid[0] * grid[1]

    def task_of(i, j):
        return jnp.minimum(i * FF_STEPS + j, n_tasks - 1)

    def page_map5(p):
        def index(i, j, pt):
            t = task_of(i, j)
            slot = n_pages - 1 - ((t % steps_per_seq) * g + p)
            return (layer, pt[seq0 + t // steps_per_seq, slot], 0, 0, 0)
        return index

    per_seq = lambda i, j, pt: (task_of(i, j) // steps_per_seq, 0, 0)
    in_specs = _ffn_in_specs(tm, layer, half, lambda i, j, pt: (i, j))
    in_specs += [
        pl.BlockSpec((1, rows, HEAD_DIM), per_seq),
        pl.BlockSpec((1, rows, HEAD_DIM), per_seq),
        pl.BlockSpec((1, rows, HEAD_DIM), per_seq),
        pl.BlockSpec((1, rows, 1), per_seq),
    ]
    in_specs += [pl.BlockSpec((1, 1, page, N_HEADS, HEAD_DIM), page_map5(p)) for p in range(g)]
    in_specs += [pl.BlockSpec((1, 1, page, N_HEADS, HEAD_DIM), page_map5(p)) for p in range(g)]
    in_specs += [pl.BlockSpec((1, 1, 2, 1, keys), page_map5(p)) for p in range(g)]
    grid_spec = pltpu.PrefetchScalarGridSpec(
        num_scalar_prefetch=1,
        grid=grid,
        in_specs=in_specs,
        out_specs=(pl.BlockSpec((tm, D_MODEL), lambda i, j, pt: (i, 0)),
                   pl.BlockSpec((1, rows, HEAD_DIM), per_seq)),
        scratch_shapes=[pltpu.VMEM((tm, D_MODEL), BF16), pltpu.VMEM((tm, D_MODEL), F32),
                        pltpu.VMEM((rows, 1), F32), pltpu.VMEM((rows, 1), F32),
                        pltpu.VMEM((rows, HEAD_DIM), F32), pltpu.VMEM((1, keys), F32)],
    )
    return pl.pallas_call(
        functools.partial(_ffn_attn_kernel, n_tasks, steps_per_seq),
        grid_spec=grid_spec,
        out_shape=(jax.ShapeDtypeStruct((m, D_MODEL), F32),
                   jax.ShapeDtypeStruct((n, rows, HEAD_DIM), F32)),
        compiler_params=pltpu.CompilerParams(dimension_semantics=("arbitrary", "arbitrary"),
                                             vmem_limit_bytes=VMEM_LIMIT_FUSED),
        name="ffn_half_attn",
    )(page_table, x, g_pre, g_post, wg, wu, wd, q, k_new, v_new, lf_new.reshape(n, rows, 1),
      *([cache_k] * g), *([cache_v] * g), *([decay] * g))


def _discretise(a_re, a_im, log_dt):
    dt = jnp.exp(log_dt)
    mag = jnp.exp(dt * a_re)
    ab_re = mag * jnp.cos(dt * a_im)
    ab_im = mag * jnp.sin(dt * a_im)
    den = a_re * a_re + a_im * a_im
    f_re = ((ab_re - 1.0) * a_re + ab_im * a_im) / den
    f_im = (ab_im * a_re - (ab_re - 1.0) * a_im) / den
    return ab_re, ab_im, f_re, f_im


def _s5_prep_kernel(pe_ref, b_ref, pf_ref, c_ref, ps_ref,
                    e_ref, f_ref, m_ref, pw_ref, e32_ref, f0_ref):
    t_len = S5_CHUNK
    sc = STATE_COLS
    ab_re, ab_im, f_re, f_im = _discretise(pe_ref[0, 0], pe_ref[0, 1], pe_ref[0, 2])
    b_re = b_ref[0, 0]
    b_im = b_ref[0, 1]
    row_g = lax.broadcasted_iota(jnp.int32, (LANES, sc), 0) // SSM_GROUP
    col_g = lax.broadcasted_iota(jnp.int32, (LANES, sc), 1) // SSM_STATE
    same = row_g == col_g
    w_re = jnp.where(same, f_re * b_re - f_im * b_im, 0.0)
    w_im = jnp.where(same, f_re * b_im + f_im * b_re, 0.0)
    for j in range(t_len):
        s = t_len - 1 - j
        e32_ref[s * LANES:(s + 1) * LANES, 0:sc] = w_re
        e32_ref[s * LANES:(s + 1) * LANES, sc:2 * sc] = w_im
        w_re, w_im = ab_re * w_re - ab_im * w_im, ab_re * w_im + ab_im * w_re
    e_ref[0] = e32_ref[...].astype(BF16)

    ab_re, ab_im, _, _ = _discretise(pf_ref[0, 0], pf_ref[0, 1], pf_ref[0, 2])
    row_g = lax.broadcasted_iota(jnp.int32, (sc, LANES), 0) // SSM_STATE
    col_g = lax.broadcasted_iota(jnp.int32, (sc, LANES), 1) // SSM_GROUP
    same = row_g == col_g
    g_re = jnp.where(same, c_ref[0, 0], 0.0)
    g_im = jnp.where(same, c_ref[0, 1], 0.0)
    f0_ref[0:sc, :] = g_re
    f0_ref[sc:2 * sc, :] = -g_im
    for t in range(t_len):
        g_re, g_im = g_re * ab_re - g_im * ab_im, g_re * ab_im + g_im * ab_re
        f_ref[0, 0:sc, t * LANES:(t + 1) * LANES] = g_re.astype(BF16)
        f_ref[0, sc:2 * sc, t * LANES:(t + 1) * LANES] = (-g_im).astype(BF16)

    lag = jnp.dot(e32_ref[...], f0_ref[...], precision=lax.Precision.HIGHEST,
                  preferred_element_type=F32).astype(BF16)
    m_ref[0] = jnp.zeros(m_ref.shape[1:], BF16)
    for t in range(t_len):
        m_ref[0, 0:(t + 1) * LANES, t * LANES:(t + 1) * LANES] = lag[(t_len - 1 - t) * LANES:, :]

    a_re, a_im, _, _ = _discretise(ps_ref[0, 0:1], ps_ref[0, 1:2], ps_ref[0, 2:3])
    a2_re, a2_im = a_re * a_re - a_im * a_im, 2.0 * a_re * a_im
    a4_re, a4_im = a2_re * a2_re - a2_im * a2_im, 2.0 * a2_re * a2_im
    a8_re, a8_im = a4_re * a4_re - a4_im * a4_im, 2.0 * a4_re * a4_im
    a16_re, a16_im = a8_re * a8_re - a8_im * a8_im, 2.0 * a8_re * a8_im
    pw_ref[0, 0:1, 0:sc] = a4_re
    pw_ref[0, 0:1, sc:2 * sc] = a4_im
    pw_ref[0, 1:2, 0:sc] = a16_re
    pw_ref[0, 1:2, sc:2 * sc] = a16_im


def _s5_prep(a_re, a_im, log_dt, b_re, b_im, c_re, c_im):
    nb, gb, st, ch = N_LANE_BLOCKS, GROUPS_PER_BLOCK, SSM_STATE, SSM_GROUP
    ldt = jnp.broadcast_to(log_dt[:, None], (SSM_GROUPS, st))
    par = jnp.stack([a_re, a_im, ldt], axis=0).reshape(3, nb, gb, st)
    pe = jnp.broadcast_to(par[:, :, :, None, None, :], (3, nb, gb, ch, gb, st))
    pe = pe.reshape(3, nb, LANES, STATE_COLS).transpose(1, 0, 2, 3)
    bb = jnp.stack([b_re, b_im], axis=0).reshape(2, nb, gb, st, ch).transpose(0, 1, 2, 4, 3)
    bb = jnp.broadcast_to(bb[:, :, :, :, None, :], (2, nb, gb, ch, gb, st))
    bb = bb.reshape(2, nb, LANES, STATE_COLS).transpose(1, 0, 2, 3)
    pf = jnp.broadcast_to(par[:, :, :, :, None, None], (3, nb, gb, st, gb, ch))
    pf = pf.reshape(3, nb, STATE_COLS, LANES).transpose(1, 0, 2, 3)
    cc = jnp.stack([c_re, c_im], axis=0).reshape(2, nb, gb, ch, st).transpose(0, 1, 2, 4, 3)
    cc = jnp.broadcast_to(cc[:, :, :, :, None, :], (2, nb, gb, st, gb, ch))
    cc = cc.reshape(2, nb, STATE_COLS, LANES).transpose(1, 0, 2, 3)
    ps = par.reshape(3, nb, STATE_COLS).transpose(1, 0, 2)

    tl = S5_CHUNK * LANES
    blk4 = lambda i: (i, 0, 0, 0)
    blk3 = lambda i: (i, 0, 0)
    return pl.pallas_call(
        _s5_prep_kernel,
        grid=(nb,),
        in_specs=[
            pl.BlockSpec((1, 3, LANES, STATE_COLS), blk4),
            pl.BlockSpec((1, 2, LANES, STATE_COLS), blk4),
            pl.BlockSpec((1, 3, STATE_COLS, LANES), blk4),
            pl.BlockSpec((1, 2, STATE_COLS, LANES), blk4),
            pl.BlockSpec((1, 3, STATE_COLS), blk3),
        ],
        out_specs=(
            pl.BlockSpec((1, tl, 2 * STATE_COLS), blk3),
            pl.BlockSpec((1, 2 * STATE_COLS, tl), blk3),
            pl.BlockSpec((1, tl, tl), blk3),
            pl.BlockSpec((1, 2, 2 * STATE_COLS), blk3),
        ),
        out_shape=(
            jax.ShapeDtypeStruct((nb, tl, 2 * STATE_COLS), BF16),
            jax.ShapeDtypeStruct((nb, 2 * STATE_COLS, tl), BF16),
            jax.ShapeDtypeStruct((nb, tl, tl), BF16),
            jax.ShapeDtypeStruct((nb, 2, 2 * STATE_COLS), F32),
        ),
        scratch_shapes=[pltpu.VMEM((tl, 2 * STATE_COLS), F32),
                        pltpu.VMEM((2 * STATE_COLS, LANES), F32)],
        compiler_params=_cparams(("parallel",)),
        name="s5_prep",
    )(pe, bb, pf, cc, ps)


def _s5_prompt_kernel(u_ref, e_ref, m_ref, f_ref, pw_ref, d_ref,
                      y_ref, hl_ref, ub_ref, sloc_ref, hp_ref, hpb_ref):
    ct = pl.program_id(1)
    n_rows = u_ref.shape[1]
    n_batch = hl_ref.shape[1]
    per_batch = n_rows // n_batch
    sc = STATE_COLS

    @pl.when(ct == 0)
    def _():
        ub = u_ref[0].astype(BF16)
        ub_ref[...] = ub
        sloc_ref[...] = jnp.dot(ub, e_ref[0], preferred_element_type=F32)
        p_re = pw_ref[0, 1:2, 0:sc]
        p_im = pw_ref[0, 1:2, sc:2 * sc]

        def body(k, carry):
            new = []
            for b in range(n_batch):
                h_re, h_im = carry[2 * b], carry[2 * b + 1]
                row = b * per_batch + k
                hp_ref[pl.ds(row, 1), 0:sc] = h_re
                hp_ref[pl.ds(row, 1), sc:2 * sc] = h_im
                s_re = sloc_ref[pl.ds(row, 1), 0:sc]
                s_im = sloc_ref[pl.ds(row, 1), sc:2 * sc]
                new.append(p_re * h_re - p_im * h_im + s_re)
                new.append(p_re * h_im + p_im * h_re + s_im)
            return tuple(new)

        zero = jnp.zeros((1, sc), F32)
        final = lax.fori_loop(0, per_batch, body, (zero,) * (2 * n_batch))
        for b in range(n_batch):
            hl_ref[0, b:b + 1, 0:sc] = final[2 * b]
            hl_ref[0, b:b + 1, sc:2 * sc] = final[2 * b + 1]
        hpb_ref[...] = hp_ref[...].astype(BF16)

    w = y_ref.shape[2]
    col = pl.multiple_of(ct * w, w)
    y = jnp.dot(ub_ref[...], m_ref[0], preferred_element_type=F32)
    y = y + jnp.dot(hpb_ref[...], f_ref[0], preferred_element_type=F32)
    y_ref[0] = y + d_ref[0] * u_ref[0, :, pl.ds(col, w)]


def _s5_prompt(u_chunks, e_op, m_op, f_op, pw, d_tiled, n_batch):
    nb, n_rows, tl = u_chunks.shape
    w = 512
    return pl.pallas_call(
        _s5_prompt_kernel,
        grid=(nb, tl // w),
        in_specs=[
            pl.BlockSpec((1, n_rows, tl), lambda i, c: (i, 0, 0)),
            pl.BlockSpec((1, tl, 2 * STATE_COLS), lambda i, c: (i, 0, 0)),
            pl.BlockSpec((1, tl, w), lambda i, c: (i, 0, c)),
            pl.BlockSpec((1, 2 * STATE_COLS, w), lambda i, c: (i, 0, c)),
            pl.BlockSpec((1, 2, 2 * STATE_COLS), lambda i, c: (i, 0, 0)),
            pl.BlockSpec((1, 1, w), lambda i, c: (i, 0, 0)),
        ],
        out_specs=(
            pl.BlockSpec((1, n_rows, w), lambda i, c: (i, 0, c)),
            pl.BlockSpec((1, n_batch, 2 * STATE_COLS), lambda i, c: (i, 0, 0)),
        ),
        out_shape=(
            jax.ShapeDtypeStruct((nb, n_rows, tl), F32),
            jax.ShapeDtypeStruct((nb, n_batch, 2 * STATE_COLS), F32),
        ),
        scratch_shapes=[pltpu.VMEM((n_rows, tl), BF16),
                        pltpu.VMEM((n_rows, 2 * STATE_COLS), F32),
                        pltpu.VMEM((n_rows, 2 * STATE_COLS), F32),
                        pltpu.VMEM((n_rows, 2 * STATE_COLS), BF16)],
        compiler_params=_cparams(("parallel", "arbitrary")),
        name="s5_prompt",
    )(u_chunks, e_op, m_op, f_op, pw, d_tiled)


def _s5_sample_kernel(u_ref, e_ref, m_ref, f_ref, pw_ref, d_ref, h0_ref, y_ref, hl_ref):
    sc = STATE_COLS
    u = u_ref[0]
    ub = u.astype(BF16)
    h0 = h0_ref[0]
    y = jnp.dot(ub, m_ref[0], preferred_element_type=F32)
    y = y + jnp.dot(h0.astype(BF16), f_ref[0], preferred_element_type=F32)
    y_ref[0] = y + d_ref[0] * u
    sloc = jnp.dot(ub, e_ref[0], preferred_element_type=F32)
    p_re = pw_ref[0, 0:1, 0:sc]
    p_im = pw_ref[0, 0:1, sc:2 * sc]
    h_re = h0[:, 0:sc]
    h_im = h0[:, sc:2 * sc]
    hl_ref[0, :, 0:sc] = p_re * h_re - p_im * h_im + sloc[:, 0:sc]
    hl_ref[0, :, sc:2 * sc] = p_re * h_im + p_im * h_re + sloc[:, sc:2 * sc]


def _s5_sample(u_chunks, e_op, m_op, f_op, pw, d_tiled, h0):
    nb, n_seq, w = u_chunks.shape
    tl = e_op.shape[1]
    e_blk = (tl - w) // w
    return pl.pallas_call(
        _s5_sample_kernel,
        grid=(nb,),
        in_specs=[
            pl.BlockSpec((1, n_seq, w), lambda i: (i, 0, 0)),
            pl.BlockSpec((1, w, 2 * STATE_COLS), lambda i: (i, e_blk, 0)),
            pl.BlockSpec((1, w, w), lambda i: (i, 0, 0)),
            pl.BlockSpec((1, 2 * STATE_COLS, w), lambda i: (i, 0, 0)),
            pl.BlockSpec((1, 2, 2 * STATE_COLS), lambda i: (i, 0, 0)),
            pl.BlockSpec((1, 1, w), lambda i: (i, 0, 0)),
            pl.BlockSpec((1, n_seq, 2 * STATE_COLS), lambda i: (i, 0, 0)),
        ],
        out_specs=(
            pl.BlockSpec((1, n_seq, w), lambda i: (i, 0, 0)),
            pl.BlockSpec((1, n_seq, 2 * STATE_COLS), lambda i: (i, 0, 0)),
        ),
        out_shape=(
            jax.ShapeDtypeStruct((nb, n_seq, w), F32),
            jax.ShapeDtypeStruct((nb, n_seq, 2 * STATE_COLS), F32),
        ),
        compiler_params=_cparams(("parallel",)),
        name="s5_sample",
    )(u_chunks, e_op, m_op, f_op, pw, d_tiled, h0)


def _mix_out_kernel(x_ref, attn_ref, y_ref, gw_ref, gb_ref, ga_ref, gs_ref, wo_ref, gp_ref,
                    o_ref, ycat_ref, mg_ref):
    for lb in range(N_LANE_BLOCKS):
        ycat_ref[:, lb * LANES:(lb + 1) * LANES] = jax.nn.gelu(y_ref[lb])
    y = ycat_ref[...]
    z = jnp.dot(y.astype(BF16), gw_ref[...], preferred_element_type=F32) + gb_ref[...]
    s = y * jax.nn.sigmoid(z)
    mg_ref[:, 0:D_ATTN] = _rms(attn_ref[...], ga_ref[...]).astype(BF16)
    mg_ref[:, D_ATTN:] = _rms(s, gs_ref[...]).astype(BF16)
    m = jnp.dot(mg_ref[...], wo_ref[...], preferred_element_type=F32)
    o_ref[...] = x_ref[...] + _rms(m, gp_ref[...])


def _mix_out(x, attn, y_blk, glu_w, glu_b, g_attn, g_ssm, w_out, g_post, layer):
    m = x.shape[0]
    tm = 256
    row = lambda i: (i, 0)
    const = lambda i: (0, 0)
    per_layer = lambda i: (layer, 0, 0)
    return pl.pallas_call(
        _mix_out_kernel,
        grid=(m // tm,),
        in_specs=[
            pl.BlockSpec((tm, D_MODEL), row),
            pl.BlockSpec((tm, D_ATTN), row),
            pl.BlockSpec((N_LANE_BLOCKS, tm, LANES), lambda i: (0, i, 0)),
            pl.BlockSpec((None, D_SSM, D_SSM), per_layer),
            pl.BlockSpec((1, D_SSM), const),
            pl.BlockSpec((1, D_ATTN), const),
            pl.BlockSpec((1, D_SSM), const),
            pl.BlockSpec((None, D_MODEL, D_MODEL), per_layer),
            pl.BlockSpec((1, D_MODEL), const),
        ],
        out_specs=pl.BlockSpec((tm, D_MODEL), row),
        out_shape=jax.ShapeDtypeStruct((m, D_MODEL), F32),
        scratch_shapes=[pltpu.VMEM((tm, D_SSM), F32), pltpu.VMEM((tm, D_MODEL), BF16)],
        compiler_params=_cparams(("parallel",)),
        name="mix_out",
    )(x, attn, y_blk, glu_w, glu_b, g_attn, g_ssm, w_out, g_post)


def _state_to_blocks(h):
    n = h.shape[0]
    h = h.reshape(n, N_LANE_BLOCKS, STATE_COLS, 2)
    return h.transpose(1, 0, 3, 2).reshape(N_LANE_BLOCKS, n, 2 * STATE_COLS)


def _blocks_to_state(h):
    n = h.shape[1]
    h = h.reshape(N_LANE_BLOCKS, n, 2, STATE_COLS).transpose(1, 0, 3, 2)
    return h.reshape(n, SSM_GROUPS, SSM_STATE, 2)


def _project_in(x, l, w):
    u_blk, q, k, v, kb, vb, lf = _mix_in(x, w["ng"][l][2], w["w_in"], w["w_f"], w["b_f"], l)
    return u_blk, q, k, v, kb, vb, lf[:, :N_HEADS]


def _project_out(x, attn, y_blk, l, w):
    return _mix_out(x, attn, y_blk, w["glu_w"], w["glu_b"][l], w["g_attn"][l], w["g_ssm"][l],
                    w["w_out"], w["ng"][l][3], l)


def kernel(x_prompt, x_sample, cache_k, cache_v, cache_logf, state_ssm, page_table, norm_g, ffn_gate, ffn_up, ffn_down, w_in, b_forget, ssm_a_re, ssm_a_im, ssm_log_dt, ssm_b_re, ssm_b_im, ssm_c_re, ssm_c_im, ssm_d, glu_w, glu_b, g_attn_out, g_ssm_out, w_out):
    batch, seq, _ = x_prompt.shape
    n_seq, n_new, _ = x_sample.shape
    depth = norm_g.shape[0]
    n_pool, page = cache_k.shape[1], cache_k.shape[2]
    assert seq % ROW_TILE == 0 and seq % ATT_TILE == 0 and seq % S5_CHUNK == 0
    assert (n_seq * n_new) % 256 == 0 and page_table.shape[1] % PAGES_PER_STEP == 0
    assert n_pool % DECAY_PAGES == 0 and N_HEADS == SUBLANES and n_seq % 2 == 0

    xp = x_prompt.reshape(batch * seq, D_MODEL)
    xs = x_sample.reshape(n_seq * n_new, D_MODEL)

    n_main = D_SSM + 3 * D_ATTN
    w = {
        "ng": [[norm_g[l, i].reshape(1, D_MODEL) for i in range(6)] for l in range(depth)],
        "wg": ffn_gate.astype(BF16),
        "wu": ffn_up.astype(BF16),
        "wd": ffn_down.astype(BF16),
        "w_in": w_in.astype(BF16),
        "w_f": jnp.pad(w_in[:, :, n_main:], ((0, 0), (0, 0), (0, LANES - N_HEADS))).astype(BF16),
        "b_f": jnp.pad(b_forget, ((0, 0), (0, LANES - N_HEADS))).reshape(depth, 1, LANES),
        "glu_w": glu_w.astype(BF16),
        "glu_b": [glu_b[l].reshape(1, D_SSM) for l in range(depth)],
        "g_attn": [g_attn_out[l].reshape(1, D_ATTN) for l in range(depth)],
        "g_ssm": [g_ssm_out[l].reshape(1, D_SSM) for l in range(depth)],
        "w_out": w_out.astype(BF16),
    }

    decay = _page_decay(cache_logf.transpose(0, 1, 3, 2))
    decay = decay.transpose(0, 2, 1, 4, 3).reshape(depth, n_pool, 2, 1, page * N_HEADS)

    outs = {name: [] for name in ("kp", "vp", "lfp", "sp", "ks", "vs", "lfs", "ss")}
    for l in range(depth):
        e_op, f_op, m_op, pw = _s5_prep(ssm_a_re[l], ssm_a_im[l], ssm_log_dt[l],
                                        ssm_b_re[l], ssm_b_im[l], ssm_c_re[l], ssm_c_im[l])
        d_blk = ssm_d[l].reshape(N_LANE_BLOCKS, 1, LANES)
        d_prompt = jnp.tile(d_blk, (1, 1, 512 // LANES))
        d_sample = jnp.tile(d_blk, (1, 1, n_new))

        ng = w["ng"][l]
        ffn_w = (w["wg"], w["wu"], w["wd"])
        rows = n_new * N_HEADS
        half_seqs = n_seq // 2

        xs = _ffn_half(xs, ng[0], ng[1], *ffn_w, l, 0)
        u2, q2, k2, v2, _, _, lf2 = _project_in(xs, l, w)
        dec = (q2.astype(F32).reshape(n_seq, rows, HEAD_DIM), k2.reshape(n_seq, rows, HEAD_DIM),
               v2.reshape(n_seq, rows, HEAD_DIM), lf2.reshape(n_seq, rows))

        def ffn_with_decode_attn(x, which, seq0):
            part = [a[seq0:seq0 + half_seqs] for a in dec]
            return _ffn_half_with_attn(x, ng[4 * which], ng[4 * which + 1], *ffn_w, l, which,
                                       *part, cache_k, cache_v, decay, page_table, seq0)

        xp, attn_a = ffn_with_decode_attn(xp, 0, 0)
        u1, q1, k1, v1, kb1, vb1, lf1 = _project_in(xp, l, w)
        lf_t = lf1.reshape(batch, seq, N_HEADS).transpose(0, 2, 1)
        c_col = _cumsum_heads(lf_t).transpose(0, 2, 1)
        attn1 = _flash_prompt(q1.reshape(batch, seq, D_ATTN), kb1.reshape(batch, seq, D_ATTN),
                              vb1.reshape(batch, seq, D_ATTN), c_col)
        u_chunks = u1.reshape(N_LANE_BLOCKS, batch * seq // S5_CHUNK, S5_CHUNK * LANES)
        y1, h1 = _s5_prompt(u_chunks, e_op, m_op, f_op, pw, d_prompt, batch)
        s1 = _blocks_to_state(h1)
        xp = _project_out(xp, attn1.reshape(batch * seq, D_ATTN),
                          y1.reshape(N_LANE_BLOCKS, batch * seq, LANES), l, w)
        xp, attn_b = ffn_with_decode_attn(xp, 1, half_seqs)

        attn2 = jnp.concatenate([attn_a, attn_b], axis=0).reshape(n_seq * n_new, D_ATTN)
        y2, h2 = _s5_sample(u2.reshape(N_LANE_BLOCKS, n_seq, n_new * LANES), e_op, m_op, f_op, pw,
                            d_sample, _state_to_blocks(state_ssm[l]))
        s2 = _blocks_to_state(h2)
        xs = _project_out(xs, attn2, y2.reshape(N_LANE_BLOCKS, n_seq * n_new, LANES), l, w)
        xs = _ffn_half(xs, ng[4], ng[5], *ffn_w, l, 1)
        outs["kp"].append(k1.reshape(batch, seq, N_HEADS, HEAD_DIM))
        outs["vp"].append(v1.reshape(batch, seq, N_HEADS, HEAD_DIM))
        outs["lfp"].append(lf1.reshape(batch, seq, N_HEADS))
        outs["sp"].append(s1)
        outs["ks"].append(k2.reshape(n_seq, n_new, N_HEADS, HEAD_DIM))
        outs["vs"].append(v2.reshape(n_seq, n_new, N_HEADS, HEAD_DIM))
        outs["lfs"].append(lf2.reshape(n_seq, n_new, N_HEADS))
        outs["ss"].append(s2)

    return (xp.reshape(batch, seq, D_MODEL), xs.reshape(n_seq, n_new, D_MODEL),
            jnp.stack(outs["kp"]), jnp.stack(outs["vp"]), jnp.stack(outs["lfp"]),
            jnp.stack(outs["sp"]),
            jnp.stack(outs["ks"]), jnp.stack(outs["vs"]), jnp.stack(outs["lfs"]),
            jnp.stack(outs["ss"]))
```

```python
import functools
import math

import jax
import jax.numpy as jnp
from jax import lax
from jax.experimental import pallas as pl
from jax.experimental.pallas import tpu as pltpu

F32 = jnp.float32
BF16 = jnp.bfloat16

D_MODEL = 2048
D_SSM = 1024
D_ATTN = 1024
HEAD_DIM = 128
N_HEADS = 8
SSM_GROUP = 16
SSM_GROUPS = 64
SSM_STATE = 64
D_FF = 5504
EPS = 1e-6
LOG2E = math.log2(math.e)

LANES = 128
SUBLANES = 8
GROUPS_PER_BLOCK = LANES // SSM_GROUP
N_LANE_BLOCKS = D_SSM // LANES
STATE_COLS = GROUPS_PER_BLOCK * SSM_STATE
S5_CHUNK = 16
FF_TILE = 256
FF_STEPS = (D_FF + FF_TILE - 1) // FF_TILE
FF_TAIL = D_FF - (FF_STEPS - 1) * FF_TILE
FFN_ROW_TILE = 1024
ROW_TILE = 512
ATT_TILE = 512
ATT_COLS = 512
HEADS_PER_STEP = 2
PAGES_PER_STEP = 8
DECAY_PAGES = 256
VMEM_LIMIT = 56 * 1024 * 1024
VMEM_LIMIT_FUSED = 60 * 1024 * 1024
NEG_BIG = -1e30


def _cparams(sem):
    return pltpu.CompilerParams(dimension_semantics=sem, vmem_limit_bytes=VMEM_LIMIT)


def _rms(x, g):
    ms = jnp.mean(x * x, axis=-1, keepdims=True)
    return x * lax.rsqrt(ms + EPS) * g


def _log_sigmoid(z):
    return jnp.minimum(z, 0.0) - jnp.log1p(jnp.exp(-jnp.abs(z)))


def _ffn_step(x_ref, gpre_ref, gpost_ref, wg_ref, wu_ref, wd_ref, o_ref, h_ref, side_work=None):
    j = pl.program_id(1)
    last = pl.num_programs(1) - 1
    acc_ref = o_ref

    @pl.when(j == 0)
    def _():
        h_ref[...] = _rms(x_ref[...], gpre_ref[...]).astype(BF16)
        acc_ref[...] = jnp.zeros_like(acc_ref)

    def hidden_tile(width):
        h = h_ref[...]
        g = jnp.dot(h, wg_ref[:, 0:width], preferred_element_type=F32)
        u = jnp.dot(h, wu_ref[:, 0:width], preferred_element_type=F32)
        a = (g * jax.nn.sigmoid(g) * u).astype(BF16)
        acc_ref[...] += jnp.dot(a, wd_ref[0:width, :], preferred_element_type=F32)
        if side_work is not None:
            side_work()

    if FF_TAIL == FF_TILE:
        hidden_tile(FF_TILE)
    else:
        pl.when(j < last)(lambda: hidden_tile(FF_TILE))
        pl.when(j == last)(lambda: hidden_tile(FF_TAIL))

    @pl.when(j == last)
    def _():
        o_ref[...] = x_ref[...] + 0.5 * _rms(acc_ref[...], gpost_ref[...])


def _ffn_in_specs(tm, layer, half, index):
    def at(f):
        return lambda *a: f(*index(*a))
    return [
        pl.BlockSpec((tm, D_MODEL), at(lambda i, j: (i, 0)), pipeline_mode=pl.Buffered(1)),
        pl.BlockSpec((1, D_MODEL), at(lambda i, j: (0, 0))),
        pl.BlockSpec((1, D_MODEL), at(lambda i, j: (0, 0))),
        pl.BlockSpec((None, None, D_MODEL, FF_TILE), at(lambda i, j: (layer, half, 0, j))),
        pl.BlockSpec((None, None, D_MODEL, FF_TILE), at(lambda i, j: (layer, half, 0, j))),
        pl.BlockSpec((None, None, FF_TILE, D_MODEL), at(lambda i, j: (layer, half, j, 0))),
    ]


def _ffn_half(x, g_pre, g_post, wg, wu, wd, layer, half):
    m = x.shape[0]
    tm = min(FFN_ROW_TILE, m)
    return pl.pallas_call(
        _ffn_step,
        grid=(m // tm, FF_STEPS),
        in_specs=_ffn_in_specs(tm, layer, half, lambda i, j: (i, j)),
        out_specs=pl.BlockSpec((tm, D_MODEL), lambda i, j: (i, 0), pipeline_mode=pl.Buffered(1)),
        out_shape=jax.ShapeDtypeStruct((m, D_MODEL), F32),
        scratch_shapes=[pltpu.VMEM((tm, D_MODEL), BF16)],
        compiler_params=_cparams(("parallel", "arbitrary")),
        name="ffn_half",
    )(x, g_pre, g_post, wg, wu, wd)


def _mix_in_kernel(x_ref, g_ref, w_ref, wf_ref, bf_ref,
                   u_ref, q_ref, k_ref, v_ref, kb_ref, vb_ref, lf_ref, h_ref):
    j = pl.program_id(1)

    @pl.when(j == 0)
    def _():
        h = _rms(x_ref[...], g_ref[...]).astype(BF16)
        h_ref[...] = h
        z = jnp.dot(h, wf_ref[...], preferred_element_type=F32) + bf_ref[...]
        lf_ref[...] = _log_sigmoid(z)

    p = jnp.dot(h_ref[...], w_ref[...], preferred_element_type=F32)

    @pl.when(j == 0)
    def _():
        for lb in range(N_LANE_BLOCKS):
            u_ref[lb] = p[:, lb * LANES:(lb + 1) * LANES]

    @pl.when(j == 1)
    def _():
        q_ref[...] = (p * (HEAD_DIM ** -0.5 * LOG2E)).astype(BF16)

    @pl.when(j == 2)
    def _():
        k_ref[...] = p
        kb_ref[...] = p.astype(BF16)

    @pl.when(j == 3)
    def _():
        v_ref[...] = p
        vb_ref[...] = p.astype(BF16)


def _mix_in(x, g, w_main, w_f, b_f, layer):
    m = x.shape[0]
    tm = min(ROW_TILE, m)
    row = lambda i, j: (i, 0)
    out_shape = (
        jax.ShapeDtypeStruct((N_LANE_BLOCKS, m, LANES), F32),
        jax.ShapeDtypeStruct((m, D_ATTN), BF16),
        jax.ShapeDtypeStruct((m, D_ATTN), F32),
        jax.ShapeDtypeStruct((m, D_ATTN), F32),
        jax.ShapeDtypeStruct((m, D_ATTN), BF16),
        jax.ShapeDtypeStruct((m, D_ATTN), BF16),
        jax.ShapeDtypeStruct((m, LANES), F32),
    )
    out_specs = (
        pl.BlockSpec((N_LANE_BLOCKS, tm, LANES), lambda i, j: (0, i, 0)),
        pl.BlockSpec((tm, D_ATTN), row),
        pl.BlockSpec((tm, D_ATTN), row),
        pl.BlockSpec((tm, D_ATTN), row),
        pl.BlockSpec((tm, D_ATTN), row),
        pl.BlockSpec((tm, D_ATTN), row),
        pl.BlockSpec((tm, LANES), row),
    )
    return pl.pallas_call(
        _mix_in_kernel,
        grid=(m // tm, 4),
        in_specs=[
            pl.BlockSpec((tm, D_MODEL), row),
            pl.BlockSpec((1, D_MODEL), lambda i, j: (0, 0)),
            pl.BlockSpec((None, D_MODEL, D_ATTN), lambda i, j: (layer, 0, j)),
            pl.BlockSpec((None, D_MODEL, LANES), lambda i, j: (layer, 0, 0)),
            pl.BlockSpec((None, 1, LANES), lambda i, j: (layer, 0, 0)),
        ],
        out_specs=out_specs,
        out_shape=out_shape,
        scratch_shapes=[pltpu.VMEM((tm, D_MODEL), BF16)],
        compiler_params=_cparams(("parallel", "arbitrary")),
        name="mix_in",
    )(x, g, w_main, w_f, b_f)


def _cumsum_kernel(x_ref, o_ref):
    blk = 512
    n = x_ref.shape[2]
    r = lax.broadcasted_iota(jnp.int32, (blk, blk), 0)
    c = lax.broadcasted_iota(jnp.int32, (blk, blk), 1)
    tri = (r <= c).astype(F32)
    carry = jnp.zeros((N_HEADS, 1), F32)
    for b in range(n // blk):
        xb = x_ref[0, :, b * blk:(b + 1) * blk]
        cb = jnp.dot(xb, tri, precision=lax.Precision.HIGHEST,
                     preferred_element_type=F32) + carry
        o_ref[0, :, b * blk:(b + 1) * blk] = cb * LOG2E
        carry = cb[:, blk - 1:blk]


def _cumsum_heads(lf_t):
    b, h, n = lf_t.shape
    return pl.pallas_call(
        _cumsum_kernel,
        grid=(b,),
        in_specs=[pl.BlockSpec((1, h, n), lambda i: (i, 0, 0))],
        out_specs=pl.BlockSpec((1, h, n), lambda i: (i, 0, 0)),
        out_shape=jax.ShapeDtypeStruct((b, h, n), F32),
        compiler_params=_cparams(("parallel",)),
        name="logf_cumsum",
    )(lf_t)


def _flash_kernel(q_ref, k_ref, v_ref, ccol_ref, o_ref, cb_ref, m_ref, l_ref, acc_ref):
    hb = pl.program_id(1)
    qi = pl.program_id(2)
    t = ATT_TILE
    n = k_ref.shape[1]

    @pl.when(qi == 0)
    def _():
        cc = ccol_ref[0]
        lane = lax.broadcasted_iota(jnp.int32, cc.shape, 1)
        for a in range(HEADS_PER_STEP):
            col = jnp.sum(jnp.where(lane == hb * HEADS_PER_STEP + a, cc, 0.0),
                          axis=1, keepdims=True)
            cb_ref[a] = jnp.broadcast_to(col, (n, LANES))

    m_ref[...] = jnp.full_like(m_ref, NEG_BIG)
    l_ref[...] = jnp.zeros_like(l_ref)
    acc_ref[...] = jnp.zeros_like(acc_ref)

    def step(ki, masked):
        start = pl.multiple_of(ki * t, t)
        for a in range(HEADS_PER_STEP):
            hl = slice(a * HEAD_DIM, (a + 1) * HEAD_DIM)
            k = k_ref[0, pl.ds(start, t), hl]
            v = v_ref[0, pl.ds(start, t), hl]
            cb = cb_ref[a, pl.ds(start, t), :]
            cb = jnp.concatenate([cb] * (ATT_COLS // LANES), axis=1)
            for j in range(t // ATT_COLS):
                cols = slice(j * ATT_COLS, (j + 1) * ATT_COLS)
                q = q_ref[0, cols, hl]
                s = lax.dot_general(k, q, (((1,), (1,)), ((), ())),
                                    preferred_element_type=F32) - cb
                if masked:
                    r = lax.broadcasted_iota(jnp.int32, s.shape, 0)
                    c = lax.broadcasted_iota(jnp.int32, s.shape, 1) + j * ATT_COLS
                    s = jnp.where(r <= c, s, NEG_BIG)
                m_old = m_ref[a, :, cols]
                m_new = jnp.maximum(m_old, jnp.max(s, axis=0, keepdims=True))
                alpha = jnp.exp2(m_old - m_new)
                p = jnp.exp2(s - m_new)
                l_ref[a, :, cols] = alpha * l_ref[a, :, cols] + jnp.sum(p, axis=0, keepdims=True)
                pv = lax.dot_general(v, p.astype(BF16), (((0,), (0,)), ((), ())),
                                     preferred_element_type=F32)
                acc_ref[a, :, cols] = alpha * acc_ref[a, :, cols] + pv
                m_ref[a, :, cols] = m_new

    def body(ki, carry):
        step(ki, False)
        return carry

    lax.fori_loop(0, qi, body, 0)
    step(qi, True)
    for a in range(HEADS_PER_STEP):
        o_ref[0, :, a * HEAD_DIM:(a + 1) * HEAD_DIM] = (acc_ref[a] / l_ref[a]).T


def _flash_prompt(q, kb, vb, c_col):
    b, n, _ = q.shape
    t = ATT_TILE
    w = HEADS_PER_STEP * HEAD_DIM
    return pl.pallas_call(
        _flash_kernel,
        grid=(b, N_HEADS // HEADS_PER_STEP, n // t),
        in_specs=[
            pl.BlockSpec((1, t, w), lambda bi, h, qi: (bi, qi, h)),
            pl.BlockSpec((1, n, w), lambda bi, h, qi: (bi, 0, h)),
            pl.BlockSpec((1, n, w), lambda bi, h, qi: (bi, 0, h)),
            pl.BlockSpec((1, n, N_HEADS), lambda bi, h, qi: (bi, 0, 0)),
        ],
        out_specs=pl.BlockSpec((1, t, w), lambda bi, h, qi: (bi, qi, h)),
        out_shape=jax.ShapeDtypeStruct((b, n, D_ATTN), F32),
        scratch_shapes=[pltpu.VMEM((HEADS_PER_STEP, n, LANES), F32),
                        pltpu.VMEM((HEADS_PER_STEP, 1, t), F32),
                        pltpu.VMEM((HEADS_PER_STEP, 1, t), F32),
                        pltpu.VMEM((HEADS_PER_STEP, HEAD_DIM, t), F32)],
        compiler_params=_cparams(("arbitrary", "arbitrary", "arbitrary")),
        name="flash_prompt",
    )(q, kb, vb, c_col)


def _page_decay_kernel(lf_ref, o_ref):
    n_pages, heads, page = lf_ref.shape[1:]
    r = lax.broadcasted_iota(jnp.int32, (page, page), 0)
    c = lax.broadcasted_iota(jnp.int32, (page, page), 1)
    later = (r > c).astype(F32)
    x = lf_ref[0].reshape(n_pages * heads, page)
    suffix = jnp.dot(x, later, precision=lax.Precision.HIGHEST, preferred_element_type=F32)
    total = jnp.sum(x, axis=1, keepdims=True)
    o_ref[0, 0] = suffix.reshape(n_pages, heads, page)
    o_ref[0, 1] = jnp.broadcast_to(total, x.shape).reshape(n_pages, heads, page)


def _page_decay(cache_lf_t):
    depth, pool, heads, page = cache_lf_t.shape
    pb = DECAY_PAGES
    return pl.pallas_call(
        _page_decay_kernel,
        grid=(depth, pool // pb),
        in_specs=[pl.BlockSpec((1, pb, heads, page), lambda d, i: (d, i, 0, 0))],
        out_specs=pl.BlockSpec((1, 2, pb, heads, page), lambda d, i: (d, 0, i, 0, 0)),
        out_shape=jax.ShapeDtypeStruct((depth, 2, pool, heads, page), F32),
        compiler_params=_cparams(("parallel", "parallel")),
        name="page_decay",
    )(cache_lf_t)


def _attn_step(active, first, last, q_ref, kn_ref, vn_ref, lfc_ref, k_refs, v_refs, d_refs, o_ref,
               m_ref, l_ref, acc_ref, carry_ref):
    g = len(k_refs)
    rows = q_ref.shape[1]
    page = k_refs[0].shape[2]
    keys = page * N_HEADS

    m_prev = m_ref[...]
    l_prev = l_ref[...]
    acc_prev = acc_ref[...]
    carry_prev = carry_ref[...]
    m_old = jnp.where(first, NEG_BIG, m_prev)
    l_old = jnp.where(first, 0.0, l_prev)
    acc_old = jnp.where(first, 0.0, acc_prev)
    carry = jnp.where(first, 0.0, carry_prev)

    qb = q_ref[0].astype(BF16)
    row_head = lax.broadcasted_iota(jnp.int32, (rows, keys), 0) % N_HEADS
    key_head = lax.broadcasted_iota(jnp.int32, (rows, keys), 1) % N_HEADS
    head_match = row_head == key_head

    scores = []
    for i in range(g):
        kb = k_refs[i][0, 0].reshape(keys, HEAD_DIM).astype(BF16)
        bias = (d_refs[i][0, 0, 0] + carry) * LOG2E
        carry = carry + d_refs[i][0, 0, 1]
        s = lax.dot_general(qb, kb, (((1,), (1,)), ((), ())), preferred_element_type=F32)
        scores.append(jnp.where(head_match, s + bias, NEG_BIG))
    values = [v_refs[i][0, 0].reshape(keys, HEAD_DIM).astype(BF16) for i in range(g)]

    kn = kn_ref[0].astype(BF16)
    s = lax.dot_general(qb, kn, (((1,), (1,)), ((), ())), preferred_element_type=F32)
    r = lax.broadcasted_iota(jnp.int32, (rows, rows), 0)
    c = lax.broadcasted_iota(jnp.int32, (rows, rows), 1)
    same_head = (r % N_HEADS) == (c % N_HEADS)
    upto = jnp.where(same_head & (r <= c), lfc_ref[0], 0.0)
    cn_row = jnp.sum(upto, axis=0, keepdims=True)
    scores.append(jnp.where(same_head & (c <= r) & last, s - cn_row * LOG2E, NEG_BIG))
    values.append(vn_ref[0].astype(BF16))

    m_new = m_old
    for s in scores:
        m_new = jnp.maximum(m_new, jnp.max(s, axis=1, keepdims=True))
    alpha = jnp.exp2(m_old - m_new)
    l_new = alpha * l_old
    acc = alpha * acc_old
    for s, v_bf in zip(scores, values):
        p = jnp.exp2(s - m_new)
        l_new = l_new + jnp.sum(p, axis=1, keepdims=True)
        acc = acc + jnp.dot(p.astype(BF16), v_bf, preferred_element_type=F32)

    m_new = jnp.where(active, m_new, m_prev)
    l_new = jnp.where(active, l_new, l_prev)
    acc = jnp.where(active, acc, acc_prev)
    m_ref[...] = m_new
    l_ref[...] = l_new
    acc_ref[...] = acc
    carry_ref[...] = jnp.where(active, carry, carry_prev)
    o_ref[0] = acc / l_new


def _ffn_attn_kernel(n_tasks, steps_per_seq, pt_ref, x_ref, gpre_ref, gpost_ref, wg_ref, wu_ref,
                     wd_ref, q_ref, kn_ref, vn_ref, lfc_ref, *rest):
    g = PAGES_PER_STEP
    k_refs = rest[0:g]
    v_refs = rest[g:2 * g]
    d_refs = rest[2 * g:3 * g]
    o_ref, oa_ref = rest[3 * g:3 * g + 2]
    h_ref, m_ref, l_ref, aacc_ref, carry_ref = rest[3 * g + 2:]
    task = pl.program_id(0) * pl.num_programs(1) + pl.program_id(1)
    part = task % steps_per_seq
    active = task < n_tasks

    @pl.when(task == 0)
    def _():
        m_ref[...] = jnp.full_like(m_ref, NEG_BIG)
        l_ref[...] = jnp.zeros_like(l_ref)
        aacc_ref[...] = jnp.zeros_like(aacc_ref)
        carry_ref[...] = jnp.zeros_like(carry_ref)

    def attention():
        _attn_step(active, active & (part == 0), active & (part == steps_per_seq - 1),
                   q_ref, kn_ref, vn_ref, lfc_ref, k_refs, v_refs, d_refs, oa_ref,
                   m_ref, l_ref, aacc_ref, carry_ref)

    _ffn_step(x_ref, gpre_ref, gpost_ref, wg_ref, wu_ref, wd_ref, o_ref, h_ref,
              side_work=attention)


def _ffn_half_with_attn(x, g_pre, g_post, wg, wu, wd, layer, half,
                        q, k_new, v_new, lf_new, cache_k, cache_v, decay, page_table, seq0):
    m = x.shape[0]
    tm = min(FFN_ROW_TILE, m)
    n, rows, _ = q.shape
    n_pages = page_table.shape[1]
    page = cache_k.shape[2]
    keys = page * N_HEADS
    g = PAGES_PER_STEP
    steps_per_seq = n_pages // g
    n_tasks = n * steps_per_seq
    grid = (m // tm, FF_STEPS)
    assert n_tasks <= grid[0] * grid[1]

    def task_of(i, j):
        return jnp.minimum(i * FF_STEPS + j, n_tasks - 1)

    def page_map5(p):
        def index(i, j, pt):
            t = task_of(i, j)
            slot = n_pages - 1 - ((t % steps_per_seq) * g + p)
            return (layer, pt[seq0 + t // steps_per_seq, slot], 0, 0, 0)
        return index

    per_seq = lambda i, j, pt: (task_of(i, j) // steps_per_seq, 0, 0)
    in_specs = _ffn_in_specs(tm, layer, half, lambda i, j, pt: (i, j))
    in_specs += [
        pl.BlockSpec((1, rows, HEAD_DIM), per_seq),
        pl.BlockSpec((1, rows, HEAD_DIM), per_seq),
        pl.BlockSpec((1, rows, HEAD_DIM), per_seq),
        pl.BlockSpec((1, rows, 1), per_seq),
    ]
    in_specs += [pl.BlockSpec((1, 1, page, N_HEADS, HEAD_DIM), page_map5(p)) for p in range(g)]
    in_specs += [pl.BlockSpec((1, 1, page, N_HEADS, HEAD_DIM), page_map5(p)) for p in range(g)]
    in_specs += [pl.BlockSpec((1, 1, 2, 1, keys), page_map5(p)) for p in range(g)]
    grid_spec = pltpu.PrefetchScalarGridSpec(
        num_scalar_prefetch=1,
        grid=grid,
        in_specs=in_specs,
        out_specs=(pl.BlockSpec((tm, D_MODEL), lambda i, j, pt: (i, 0),
                                pipeline_mode=pl.Buffered(1)),
                   pl.BlockSpec((1, rows, HEAD_DIM), per_seq)),
        scratch_shapes=[pltpu.VMEM((tm, D_MODEL), BF16),
                        pltpu.VMEM((rows, 1), F32), pltpu.VMEM((rows, 1), F32),
                        pltpu.VMEM((rows, HEAD_DIM), F32), pltpu.VMEM((1, keys), F32)],
    )
    return pl.pallas_call(
        functools.partial(_ffn_attn_kernel, n_tasks, steps_per_seq),
        grid_spec=grid_spec,
        out_shape=(jax.ShapeDtypeStruct((m, D_MODEL), F32),
                   jax.ShapeDtypeStruct((n, rows, HEAD_DIM), F32)),
        compiler_params=pltpu.CompilerParams(dimension_semantics=("arbitrary", "arbitrary"),
                                             vmem_limit_bytes=VMEM_LIMIT_FUSED),
        name="ffn_half_attn",
    )(page_table, x, g_pre, g_post, wg, wu, wd, q, k_new, v_new, lf_new.reshape(n, rows, 1),
      *([cache_k] * g), *([cache_v] * g), *([decay] * g))


def _discretise(a_re, a_im, log_dt):
    dt = jnp.exp(log_dt)
    mag = jnp.exp(dt * a_re)
    ab_re = mag * jnp.cos(dt * a_im)
    ab_im = mag * jnp.sin(dt * a_im)
    den = a_re * a_re + a_im * a_im
    f_re = ((ab_re - 1.0) * a_re + ab_im * a_im) / den
    f_im = (ab_im * a_re - (ab_re - 1.0) * a_im) / den
    return ab_re, ab_im, f_re, f_im


def _s5_prep_kernel(pe_ref, b_ref, pf_ref, c_ref, ps_ref,
                    e_ref, f_ref, m_ref, pw_ref, e32_ref, f0_ref):
    t_len = S5_CHUNK
    sc = STATE_COLS
    ab_re, ab_im, f_re, f_im = _discretise(pe_ref[0, 0], pe_ref[0, 1], pe_ref[0, 2])
    b_re = b_ref[0, 0]
    b_im = b_ref[0, 1]
    row_g = lax.broadcasted_iota(jnp.int32, (LANES, sc), 0) // SSM_GROUP
    col_g = lax.broadcasted_iota(jnp.int32, (LANES, sc), 1) // SSM_STATE
    same = row_g == col_g
    w_re = jnp.where(same, f_re * b_re - f_im * b_im, 0.0)
    w_im = jnp.where(same, f_re * b_im + f_im * b_re, 0.0)
    for j in range(t_len):
        s = t_len - 1 - j
        e32_ref[s * LANES:(s + 1) * LANES, 0:sc] = w_re
        e32_ref[s * LANES:(s + 1) * LANES, sc:2 * sc] = w_im
        w_re, w_im = ab_re * w_re - ab_im * w_im, ab_re * w_im + ab_im * w_re
    e_ref[0] = e32_ref[...].astype(BF16)

    ab_re, ab_im, _, _ = _discretise(pf_ref[0, 0], pf_ref[0, 1], pf_ref[0, 2])
    row_g = lax.broadcasted_iota(jnp.int32, (sc, LANES), 0) // SSM_STATE
    col_g = lax.broadcasted_iota(jnp.int32, (sc, LANES), 1) // SSM_GROUP
    same = row_g == col_g
    g_re = jnp.where(same, c_ref[0, 0], 0.0)
    g_im = jnp.where(same, c_ref[0, 1], 0.0)
    f0_ref[0:sc, :] = g_re
    f0_ref[sc:2 * sc, :] = -g_im
    for t in range(t_len):
        g_re, g_im = g_re * ab_re - g_im * ab_im, g_re * ab_im + g_im * ab_re
        f_ref[0, 0:sc, t * LANES:(t + 1) * LANES] = g_re.astype(BF16)
        f_ref[0, sc:2 * sc, t * LANES:(t + 1) * LANES] = (-g_im).astype(BF16)

    lag = jnp.dot(e32_ref[...], f0_ref[...], precision=lax.Precision.HIGHEST,
                  preferred_element_type=F32).astype(BF16)
    m_ref[0] = jnp.zeros(m_ref.shape[1:], BF16)
    for t in range(t_len):
        m_ref[0, 0:(t + 1) * LANES, t * LANES:(t + 1) * LANES] = lag[(t_len - 1 - t) * LANES:, :]

    a_re, a_im, _, _ = _discretise(ps_ref[0, 0:1], ps_ref[0, 1:2], ps_ref[0, 2:3])
    a2_re, a2_im = a_re * a_re - a_im * a_im, 2.0 * a_re * a_im
    a4_re, a4_im = a2_re * a2_re - a2_im * a2_im, 2.0 * a2_re * a2_im
    a8_re, a8_im = a4_re * a4_re - a4_im * a4_im, 2.0 * a4_re * a4_im
    a16_re, a16_im = a8_re * a8_re - a8_im * a8_im, 2.0 * a8_re * a8_im
    pw_ref[0, 0:1, 0:sc] = a4_re
    pw_ref[0, 0:1, sc:2 * sc] = a4_im
    pw_ref[0, 1:2, 0:sc] = a16_re
    pw_ref[0, 1:2, sc:2 * sc] = a16_im


def _s5_prep(a_re, a_im, log_dt, b_re, b_im, c_re, c_im):
    nb, gb, st, ch = N_LANE_BLOCKS, GROUPS_PER_BLOCK, SSM_STATE, SSM_GROUP
    ldt = jnp.broadcast_to(log_dt[:, None], (SSM_GROUPS, st))
    par = jnp.stack([a_re, a_im, ldt], axis=0).reshape(3, nb, gb, st)
    pe = jnp.broadcast_to(par[:, :, :, None, None, :], (3, nb, gb, ch, gb, st))
    pe = pe.reshape(3, nb, LANES, STATE_COLS).transpose(1, 0, 2, 3)
    bb = jnp.stack([b_re, b_im], axis=0).reshape(2, nb, gb, st, ch).transpose(0, 1, 2, 4, 3)
    bb = jnp.broadcast_to(bb[:, :, :, :, None, :], (2, nb, gb, ch, gb, st))
    bb = bb.reshape(2, nb, LANES, STATE_COLS).transpose(1, 0, 2, 3)
    pf = jnp.broadcast_to(par[:, :, :, :, None, None], (3, nb, gb, st, gb, ch))
    pf = pf.reshape(3, nb, STATE_COLS, LANES).transpose(1, 0, 2, 3)
    cc = jnp.stack([c_re, c_im], axis=0).reshape(2, nb, gb, ch, st).transpose(0, 1, 2, 4, 3)
    cc = jnp.broadcast_to(cc[:, :, :, :, None, :], (2, nb, gb, st, gb, ch))
    cc = cc.reshape(2, nb, STATE_COLS, LANES).transpose(1, 0, 2, 3)
    ps = par.reshape(3, nb, STATE_COLS).transpose(1, 0, 2)

    tl = S5_CHUNK * LANES
    blk4 = lambda i: (i, 0, 0, 0)
    blk3 = lambda i: (i, 0, 0)
    return pl.pallas_call(
        _s5_prep_kernel,
        grid=(nb,),
        in_specs=[
            pl.BlockSpec((1, 3, LANES, STATE_COLS), blk4),
            pl.BlockSpec((1, 2, LANES, STATE_COLS), blk4),
            pl.BlockSpec((1, 3, STATE_COLS, LANES), blk4),
            pl.BlockSpec((1, 2, STATE_COLS, LANES), blk4),
            pl.BlockSpec((1, 3, STATE_COLS), blk3),
        ],
        out_specs=(
            pl.BlockSpec((1, tl, 2 * STATE_COLS), blk3),
            pl.BlockSpec((1, 2 * STATE_COLS, tl), blk3),
            pl.BlockSpec((1, tl, tl), blk3),
            pl.BlockSpec((1, 2, 2 * STATE_COLS), blk3),
        ),
        out_shape=(
            jax.ShapeDtypeStruct((nb, tl, 2 * STATE_COLS), BF16),
            jax.ShapeDtypeStruct((nb, 2 * STATE_COLS, tl), BF16),
            jax.ShapeDtypeStruct((nb, tl, tl), BF16),
            jax.ShapeDtypeStruct((nb, 2, 2 * STATE_COLS), F32),
        ),
        scratch_shapes=[pltpu.VMEM((tl, 2 * STATE_COLS), F32),
                        pltpu.VMEM((2 * STATE_COLS, LANES), F32)],
        compiler_params=_cparams(("parallel",)),
        name="s5_prep",
    )(pe, bb, pf, cc, ps)


def _s5_prompt_kernel(u_ref, e_ref, m_ref, f_ref, pw_ref, d_ref,
                      y_ref, hl_ref, ub_ref, sloc_ref, hp_ref, hpb_ref):
    ct = pl.program_id(1)
    n_rows = u_ref.shape[1]
    n_batch = hl_ref.shape[1]
    per_batch = n_rows // n_batch
    sc = STATE_COLS

    @pl.when(ct == 0)
    def _():
        ub = u_ref[0].astype(BF16)
        ub_ref[...] = ub
        sloc_ref[...] = jnp.dot(ub, e_ref[0], preferred_element_type=F32)
        p_re = pw_ref[0, 1:2, 0:sc]
        p_im = pw_ref[0, 1:2, sc:2 * sc]

        def body(k, carry):
            new = []
            for b in range(n_batch):
                h_re, h_im = carry[2 * b], carry[2 * b + 1]
                row = b * per_batch + k
                hp_ref[pl.ds(row, 1), 0:sc] = h_re
                hp_ref[pl.ds(row, 1), sc:2 * sc] = h_im
                s_re = sloc_ref[pl.ds(row, 1), 0:sc]
                s_im = sloc_ref[pl.ds(row, 1), sc:2 * sc]
                new.append(p_re * h_re - p_im * h_im + s_re)
                new.append(p_re * h_im + p_im * h_re + s_im)
            return tuple(new)

        zero = jnp.zeros((1, sc), F32)
        final = lax.fori_loop(0, per_batch, body, (zero,) * (2 * n_batch))
        for b in range(n_batch):
            hl_ref[0, b:b + 1, 0:sc] = final[2 * b]
            hl_ref[0, b:b + 1, sc:2 * sc] = final[2 * b + 1]
        hpb_ref[...] = hp_ref[...].astype(BF16)

    w = y_ref.shape[2]
    col = pl.multiple_of(ct * w, w)
    y = jnp.dot(ub_ref[...], m_ref[0], preferred_element_type=F32)
    y = y + jnp.dot(hpb_ref[...], f_ref[0], preferred_element_type=F32)
    y_ref[0] = y + d_ref[0] * u_ref[0, :, pl.ds(col, w)]


def _s5_prompt(u_chunks, e_op, m_op, f_op, pw, d_tiled, n_batch):
    nb, n_rows, tl = u_chunks.shape
    w = 512
    return pl.pallas_call(
        _s5_prompt_kernel,
        grid=(nb, tl // w),
        in_specs=[
            pl.BlockSpec((1, n_rows, tl), lambda i, c: (i, 0, 0)),
            pl.BlockSpec((1, tl, 2 * STATE_COLS), lambda i, c: (i, 0, 0)),
            pl.BlockSpec((1, tl, w), lambda i, c: (i, 0, c)),
            pl.BlockSpec((1, 2 * STATE_COLS, w), lambda i, c: (i, 0, c)),
            pl.BlockSpec((1, 2, 2 * STATE_COLS), lambda i, c: (i, 0, 0)),
            pl.BlockSpec((1, 1, w), lambda i, c: (i, 0, 0)),
        ],
        out_specs=(
            pl.BlockSpec((1, n_rows, w), lambda i, c: (i, 0, c)),
            pl.BlockSpec((1, n_batch, 2 * STATE_COLS), lambda i, c: (i, 0, 0)),
        ),
        out_shape=(
            jax.ShapeDtypeStruct((nb, n_rows, tl), F32),
            jax.ShapeDtypeStruct((nb, n_batch, 2 * STATE_COLS), F32),
        ),
        scratch_shapes=[pltpu.VMEM((n_rows, tl), BF16),
                        pltpu.VMEM((n_rows, 2 * STATE_COLS), F32),
                        pltpu.VMEM((n_rows, 2 * STATE_COLS), F32),
                        pltpu.VMEM((n_rows, 2 * STATE_COLS), BF16)],
        compiler_params=_cparams(("parallel", "arbitrary")),
        name="s5_prompt",
    )(u_chunks, e_op, m_op, f_op, pw, d_tiled)


def _s5_sample_kernel(u_ref, e_ref, m_ref, f_ref, pw_ref, d_ref, h0_ref, y_ref, hl_ref):
    sc = STATE_COLS
    u = u_ref[0]
    ub = u.astype(BF16)
    h0 = h0_ref[0]
    y = jnp.dot(ub, m_ref[0], preferred_element_type=F32)
    y = y + jnp.dot(h0.astype(BF16), f_ref[0], preferred_element_type=F32)
    y_ref[0] = y + d_ref[0] * u
    sloc = jnp.dot(ub, e_ref[0], preferred_element_type=F32)
    p_re = pw_ref[0, 0:1, 0:sc]
    p_im = pw_ref[0, 0:1, sc:2 * sc]
    h_re = h0[:, 0:sc]
    h_im = h0[:, sc:2 * sc]
    hl_ref[0, :, 0:sc] = p_re * h_re - p_im * h_im + sloc[:, 0:sc]
    hl_ref[0, :, sc:2 * sc] = p_re * h_im + p_im * h_re + sloc[:, sc:2 * sc]


def _s5_sample(u_chunks, e_op, m_op, f_op, pw, d_tiled, h0):
    nb, n_seq, w = u_chunks.shape
    tl = e_op.shape[1]
    e_blk = (tl - w) // w
    return pl.pallas_call(
        _s5_sample_kernel,
        grid=(nb,),
        in_specs=[
            pl.BlockSpec((1, n_seq, w), lambda i: (i, 0, 0)),
            pl.BlockSpec((1, w, 2 * STATE_COLS), lambda i: (i, e_blk, 0)),
            pl.BlockSpec((1, w, w), lambda i: (i, 0, 0)),
            pl.BlockSpec((1, 2 * STATE_COLS, w), lambda i: (i, 0, 0)),
            pl.BlockSpec((1, 2, 2 * STATE_COLS), lambda i: (i, 0, 0)),
            pl.BlockSpec((1, 1, w), lambda i: (i, 0, 0)),
            pl.BlockSpec((1, n_seq, 2 * STATE_COLS), lambda i: (i, 0, 0)),
        ],
        out_specs=(
            pl.BlockSpec((1, n_seq, w), lambda i: (i, 0, 0)),
            pl.BlockSpec((1, n_seq, 2 * STATE_COLS), lambda i: (i, 0, 0)),
        ),
        out_shape=(
            jax.ShapeDtypeStruct((nb, n_seq, w), F32),
            jax.ShapeDtypeStruct((nb, n_seq, 2 * STATE_COLS), F32),
        ),
        compiler_params=_cparams(("parallel",)),
        name="s5_sample",
    )(u_chunks, e_op, m_op, f_op, pw, d_tiled, h0)


def _mix_out_kernel(x_ref, attn_ref, y_ref, gw_ref, gb_ref, ga_ref, gs_ref, wo_ref, gp_ref,
                    o_ref, ycat_ref, mg_ref):
    for lb in range(N_LANE_BLOCKS):
        ycat_ref[:, lb * LANES:(lb + 1) * LANES] = jax.nn.gelu(y_ref[lb])
    y = ycat_ref[...]
    z = jnp.dot(y.astype(BF16), gw_ref[...], preferred_element_type=F32) + gb_ref[...]
    s = y * jax.nn.sigmoid(z)
    mg_ref[:, 0:D_ATTN] = _rms(attn_ref[...], ga_ref[...]).astype(BF16)
    mg_ref[:, D_ATTN:] = _rms(s, gs_ref[...]).astype(BF16)
    m = jnp.dot(mg_ref[...], wo_ref[...], preferred_element_type=F32)
    o_ref[...] = x_ref[...] + _rms(m, gp_ref[...])


def _mix_out(x, attn, y_blk, glu_w, glu_b, g_attn, g_ssm, w_out, g_post, layer):
    m = x.shape[0]
    tm = 256
    row = lambda i: (i, 0)
    const = lambda i: (0, 0)
    per_layer = lambda i: (layer, 0, 0)
    return pl.pallas_call(
        _mix_out_kernel,
        grid=(m // tm,),
        in_specs=[
            pl.BlockSpec((tm, D_MODEL), row),
            pl.BlockSpec((tm, D_ATTN), row),
            pl.BlockSpec((N_LANE_BLOCKS, tm, LANES), lambda i: (0, i, 0)),
            pl.BlockSpec((None, D_SSM, D_SSM), per_layer),
            pl.BlockSpec((1, D_SSM), const),
            pl.BlockSpec((1, D_ATTN), const),
            pl.BlockSpec((1, D_SSM), const),
            pl.BlockSpec((None, D_MODEL, D_MODEL), per_layer),
            pl.BlockSpec((1, D_MODEL), const),
        ],
        out_specs=pl.BlockSpec((tm, D_MODEL), row),
        out_shape=jax.ShapeDtypeStruct((m, D_MODEL), F32),
        scratch_shapes=[pltpu.VMEM((tm, D_SSM), F32), pltpu.VMEM((tm, D_MODEL), BF16)],
        compiler_params=_cparams(("parallel",)),
        name="mix_out",
    )(x, attn, y_blk, glu_w, glu_b, g_attn, g_ssm, w_out, g_post)


def _state_to_blocks(h):
    n = h.shape[0]
    h = h.reshape(n, N_LANE_BLOCKS, STATE_COLS, 2)
    return h.transpose(1, 0, 3, 2).reshape(N_LANE_BLOCKS, n, 2 * STATE_COLS)


def _blocks_to_state(h):
    n = h.shape[1]
    h = h.reshape(N_LANE_BLOCKS, n, 2, STATE_COLS).transpose(1, 0, 3, 2)
    return h.reshape(n, SSM_GROUPS, SSM_STATE, 2)


def _project_in(x, l, w):
    u_blk, q, k, v, kb, vb, lf = _mix_in(x, w["ng"][l][2], w["w_in"], w["w_f"], w["b_f"], l)
    return u_blk, q, k, v, kb, vb, lf[:, :N_HEADS]


def _project_out(x, attn, y_blk, l, w):
    return _mix_out(x, attn, y_blk, w["glu_w"], w["glu_b"][l], w["g_attn"][l], w["g_ssm"][l],
                    w["w_out"], w["ng"][l][3], l)


def kernel(x_prompt, x_sample, cache_k, cache_v, cache_logf, state_ssm, page_table, norm_g, ffn_gate, ffn_up, ffn_down, w_in, b_forget, ssm_a_re, ssm_a_im, ssm_log_dt, ssm_b_re, ssm_b_im, ssm_c_re, ssm_c_im, ssm_d, glu_w, glu_b, g_attn_out, g_ssm_out, w_out):
    batch, seq, _ = x_prompt.shape
    n_seq, n_new, _ = x_sample.shape
    depth = norm_g.shape[0]
    n_pool, page = cache_k.shape[1], cache_k.shape[2]
    assert seq % ROW_TILE == 0 and seq % ATT_TILE == 0 and seq % S5_CHUNK == 0
    assert (n_seq * n_new) % 256 == 0 and page_table.shape[1] % PAGES_PER_STEP == 0
    assert n_pool % DECAY_PAGES == 0 and N_HEADS == SUBLANES and n_seq % 2 == 0

    xp = x_prompt.reshape(batch * seq, D_MODEL)
    xs = x_sample.reshape(n_seq * n_new, D_MODEL)

    n_main = D_SSM + 3 * D_ATTN
    w = {
        "ng": [[norm_g[l, i].reshape(1, D_MODEL) for i in range(6)] for l in range(depth)],
        "wg": ffn_gate.astype(BF16),
        "wu": ffn_up.astype(BF16),
        "wd": ffn_down.astype(BF16),
        "w_in": w_in.astype(BF16),
        "w_f": jnp.pad(w_in[:, :, n_main:], ((0, 0), (0, 0), (0, LANES - N_HEADS))).astype(BF16),
        "b_f": jnp.pad(b_forget, ((0, 0), (0, LANES - N_HEADS))).reshape(depth, 1, LANES),
        "glu_w": glu_w.astype(BF16),
        "glu_b": [glu_b[l].reshape(1, D_SSM) for l in range(depth)],
        "g_attn": [g_attn_out[l].reshape(1, D_ATTN) for l in range(depth)],
        "g_ssm": [g_ssm_out[l].reshape(1, D_SSM) for l in range(depth)],
        "w_out": w_out.astype(BF16),
    }

    decay = _page_decay(cache_logf.transpose(0, 1, 3, 2))
    decay = decay.transpose(0, 2, 1, 4, 3).reshape(depth, n_pool, 2, 1, page * N_HEADS)

    outs = {name: [] for name in ("kp", "vp", "lfp", "sp", "ks", "vs", "lfs", "ss")}
    for l in range(depth):
        e_op, f_op, m_op, pw = _s5_prep(ssm_a_re[l], ssm_a_im[l], ssm_log_dt[l],
                                        ssm_b_re[l], ssm_b_im[l], ssm_c_re[l], ssm_c_im[l])
        d_blk = ssm_d[l].reshape(N_LANE_BLOCKS, 1, LANES)
        d_prompt = jnp.tile(d_blk, (1, 1, 512 // LANES))
        d_sample = jnp.tile(d_blk, (1, 1, n_new))

        ng = w["ng"][l]
        ffn_w = (w["wg"], w["wu"], w["wd"])
        rows = n_new * N_HEADS
        half_seqs = n_seq // 2

        xs = _ffn_half(xs, ng[0], ng[1], *ffn_w, l, 0)
        u2, q2, k2, v2, _, _, lf2 = _project_in(xs, l, w)
        dec = (q2.astype(F32).reshape(n_seq, rows, HEAD_DIM), k2.reshape(n_seq, rows, HEAD_DIM),
               v2.reshape(n_seq, rows, HEAD_DIM), lf2.reshape(n_seq, rows))

        def ffn_with_decode_attn(x, which, seq0):
            part = [a[seq0:seq0 + half_seqs] for a in dec]
            return _ffn_half_with_attn(x, ng[4 * which], ng[4 * which + 1], *ffn_w, l, which,
                                       *part, cache_k, cache_v, decay, page_table, seq0)

        xp, attn_a = ffn_with_decode_attn(xp, 0, 0)
        u1, q1, k1, v1, kb1, vb1, lf1 = _project_in(xp, l, w)
        lf_t = lf1.reshape(batch, seq, N_HEADS).transpose(0, 2, 1)
        c_col = _cumsum_heads(lf_t).transpose(0, 2, 1)
        attn1 = _flash_prompt(q1.reshape(batch, seq, D_ATTN), kb1.reshape(batch, seq, D_ATTN),
                              vb1.reshape(batch, seq, D_ATTN), c_col)
        u_chunks = u1.reshape(N_LANE_BLOCKS, batch * seq // S5_CHUNK, S5_CHUNK * LANES)
        y1, h1 = _s5_prompt(u_chunks, e_op, m_op, f_op, pw, d_prompt, batch)
        s1 = _blocks_to_state(h1)
        xp = _project_out(xp, attn1.reshape(batch * seq, D_ATTN),
                          y1.reshape(N_LANE_BLOCKS, batch * seq, LANES), l, w)
        xp, attn_b = ffn_with_decode_attn(xp, 1, half_seqs)

        attn2 = jnp.concatenate([attn_a, attn_b], axis=0).reshape(n_seq * n_new, D_ATTN)
        y2, h2 = _s5_sample(u2.reshape(N_LANE_BLOCKS, n_seq, n_new * LANES), e_op, m_op, f_op, pw,
                            d_sample, _state_to_blocks(state_ssm[l]))
        s2 = _blocks_to_state(h2)
        xs = _project_out(xs, attn2, y2.reshape(N_LANE_BLOCKS, n_seq * n_new, LANES), l, w)
        xs = _ffn_half(xs, ng[4], ng[5], *ffn_w, l, 1)
        outs["kp"].append(k1.reshape(batch, seq, N_HEADS, HEAD_DIM))
        outs["vp"].append(v1.reshape(batch, seq, N_HEADS, HEAD_DIM))
        outs["lfp"].append(lf1.reshape(batch, seq, N_HEADS))
        outs["sp"].append(s1)
        outs["ks"].append(k2.reshape(n_seq, n_new, N_HEADS, HEAD_DIM))
        outs["vs"].append(v2.reshape(n_seq, n_new, N_HEADS, HEAD_DIM))
        outs["lfs"].append(lf2.reshape(n_seq, n_new, N_HEADS))
        outs["ss"].append(s2)

    return (xp.reshape(batch, seq, D_MODEL), xs.reshape(n_seq, n_new, D_MODEL),
            jnp.stack(outs["kp"]), jnp.stack(outs["vp"]), jnp.stack(outs["lfp"]),
            jnp.stack(outs["sp"]),
            jnp.stack(outs["ks"]), jnp.stack(outs["vs"]), jnp.stack(outs["lfs"]),
            jnp.stack(outs["ss"]))
```

```python
import math

import jax
import jax.numpy as jnp
from jax import lax
from jax.experimental import pallas as pl
from jax.experimental.pallas import tpu as pltpu

F32 = jnp.float32
BF16 = jnp.bfloat16

D_MODEL = 2048
D_SSM = 1024
D_ATTN = 1024
HEAD_DIM = 128
N_HEADS = 8
SSM_GROUP = 16
SSM_GROUPS = 64
SSM_STATE = 64
D_FF = 5504
EPS = 1e-6
LOG2E = math.log2(math.e)

LANES = 128
SUBLANES = 8
GROUPS_PER_BLOCK = LANES // SSM_GROUP
N_LANE_BLOCKS = D_SSM // LANES
STATE_COLS = GROUPS_PER_BLOCK * SSM_STATE
S5_CHUNK = 16
FF_TILE = 1024
FF_STEPS = (D_FF + FF_TILE - 1) // FF_TILE
FF_TAIL = D_FF - (FF_STEPS - 1) * FF_TILE
ROW_TILE = 512
ATT_TILE = 512
ATT_COLS = 512
HEADS_PER_STEP = 2
PAGES_PER_SEQ = 16
DECAY_PAGES = 256
VMEM_LIMIT = 56 * 1024 * 1024
NEG_BIG = -1e30


def _cparams(sem):
    return pltpu.CompilerParams(dimension_semantics=sem, vmem_limit_bytes=VMEM_LIMIT)


def _rms(x, g):
    ms = jnp.mean(x * x, axis=-1, keepdims=True)
    return x * lax.rsqrt(ms + EPS) * g


def _log_sigmoid(z):
    return jnp.minimum(z, 0.0) - jnp.log1p(jnp.exp(-jnp.abs(z)))


def _ffn_kernel(x_ref, gpre_ref, gpost_ref, wg_ref, wu_ref, wd_ref, o_ref, h_ref):
    j = pl.program_id(1)
    last = pl.num_programs(1) - 1
    acc_ref = o_ref

    @pl.when(j == 0)
    def _():
        h_ref[...] = _rms(x_ref[...], gpre_ref[...]).astype(BF16)
        acc_ref[...] = jnp.zeros_like(acc_ref)

    def hidden_tile(width):
        h = h_ref[...]
        g = jnp.dot(h, wg_ref[:, 0:width], preferred_element_type=F32)
        u = jnp.dot(h, wu_ref[:, 0:width], preferred_element_type=F32)
        a = (g * jax.nn.sigmoid(g) * u).astype(BF16)
        acc_ref[...] += jnp.dot(a, wd_ref[0:width, :], preferred_element_type=F32)

    if FF_TAIL == FF_TILE:
        hidden_tile(FF_TILE)
    else:
        pl.when(j < last)(lambda: hidden_tile(FF_TILE))
        pl.when(j == last)(lambda: hidden_tile(FF_TAIL))

    @pl.when(j == last)
    def _():
        o_ref[...] = x_ref[...] + 0.5 * _rms(acc_ref[...], gpost_ref[...])


def _ffn_half(x, g_pre, g_post, wg, wu, wd, layer, half):
    m = x.shape[0]
    tm = min(ROW_TILE, m)
    return pl.pallas_call(
        _ffn_kernel,
        grid=(m // tm, FF_STEPS),
        in_specs=[
            pl.BlockSpec((tm, D_MODEL), lambda i, j: (i, 0)),
            pl.BlockSpec((1, D_MODEL), lambda i, j: (0, 0)),
            pl.BlockSpec((1, D_MODEL), lambda i, j: (0, 0)),
            pl.BlockSpec((None, None, D_MODEL, FF_TILE), lambda i, j: (layer, half, 0, j)),
            pl.BlockSpec((None, None, D_MODEL, FF_TILE), lambda i, j: (layer, half, 0, j)),
            pl.BlockSpec((None, None, FF_TILE, D_MODEL), lambda i, j: (layer, half, j, 0)),
        ],
        out_specs=pl.BlockSpec((tm, D_MODEL), lambda i, j: (i, 0)),
        out_shape=jax.ShapeDtypeStruct((m, D_MODEL), F32),
        scratch_shapes=[pltpu.VMEM((tm, D_MODEL), BF16)],
        compiler_params=_cparams(("parallel", "arbitrary")),
        name="ffn_half",
    )(x, g_pre, g_post, wg, wu, wd)


def _mix_in_kernel(x_ref, g_ref, w_ref, wf_ref, bf_ref,
                   u_ref, q_ref, k_ref, v_ref, kb_ref, vb_ref, lf_ref, h_ref):
    j = pl.program_id(1)

    @pl.when(j == 0)
    def _():
        h = _rms(x_ref[...], g_ref[...]).astype(BF16)
        h_ref[...] = h
        z = jnp.dot(h, wf_ref[...], preferred_element_type=F32) + bf_ref[...]
        lf_ref[...] = _log_sigmoid(z)

    p = jnp.dot(h_ref[...], w_ref[...], preferred_element_type=F32)

    @pl.when(j == 0)
    def _():
        for lb in range(N_LANE_BLOCKS):
            u_ref[lb] = p[:, lb * LANES:(lb + 1) * LANES]

    @pl.when(j == 1)
    def _():
        q_ref[...] = (p * (HEAD_DIM ** -0.5 * LOG2E)).astype(BF16)

    @pl.when(j == 2)
    def _():
        k_ref[...] = p
        kb_ref[...] = p.astype(BF16)

    @pl.when(j == 3)
    def _():
        v_ref[...] = p
        vb_ref[...] = p.astype(BF16)


def _mix_in(x, g, w_main, w_f, b_f, layer):
    m = x.shape[0]
    tm = min(ROW_TILE, m)
    row = lambda i, j: (i, 0)
    out_shape = (
        jax.ShapeDtypeStruct((N_LANE_BLOCKS, m, LANES), F32),
        jax.ShapeDtypeStruct((m, D_ATTN), BF16),
        jax.ShapeDtypeStruct((m, D_ATTN), F32),
        jax.ShapeDtypeStruct((m, D_ATTN), F32),
        jax.ShapeDtypeStruct((m, D_ATTN), BF16),
        jax.ShapeDtypeStruct((m, D_ATTN), BF16),
        jax.ShapeDtypeStruct((m, LANES), F32),
    )
    out_specs = (
        pl.BlockSpec((N_LANE_BLOCKS, tm, LANES), lambda i, j: (0, i, 0)),
        pl.BlockSpec((tm, D_ATTN), row),
        pl.BlockSpec((tm, D_ATTN), row),
        pl.BlockSpec((tm, D_ATTN), row),
        pl.BlockSpec((tm, D_ATTN), row),
        pl.BlockSpec((tm, D_ATTN), row),
        pl.BlockSpec((tm, LANES), row),
    )
    return pl.pallas_call(
        _mix_in_kernel,
        grid=(m // tm, 4),
        in_specs=[
            pl.BlockSpec((tm, D_MODEL), row),
            pl.BlockSpec((1, D_MODEL), lambda i, j: (0, 0)),
            pl.BlockSpec((None, D_MODEL, D_ATTN), lambda i, j: (layer, 0, j)),
            pl.BlockSpec((None, D_MODEL, LANES), lambda i, j: (layer, 0, 0)),
            pl.BlockSpec((None, 1, LANES), lambda i, j: (layer, 0, 0)),
        ],
        out_specs=out_specs,
        out_shape=out_shape,
        scratch_shapes=[pltpu.VMEM((tm, D_MODEL), BF16)],
        compiler_params=_cparams(("parallel", "arbitrary")),
        name="mix_in",
    )(x, g, w_main, w_f, b_f)


def _cumsum_kernel(x_ref, o_ref):
    blk = 512
    n = x_ref.shape[2]
    r = lax.broadcasted_iota(jnp.int32, (blk, blk), 0)
    c = lax.broadcasted_iota(jnp.int32, (blk, blk), 1)
    tri = (r <= c).astype(F32)
    carry = jnp.zeros((N_HEADS, 1), F32)
    for b in range(n // blk):
        xb = x_ref[0, :, b * blk:(b + 1) * blk]
        cb = jnp.dot(xb, tri, precision=lax.Precision.HIGHEST,
                     preferred_element_type=F32) + carry
        o_ref[0, :, b * blk:(b + 1) * blk] = cb * LOG2E
        carry = cb[:, blk - 1:blk]


def _cumsum_heads(lf_t):
    b, h, n = lf_t.shape
    return pl.pallas_call(
        _cumsum_kernel,
        grid=(b,),
        in_specs=[pl.BlockSpec((1, h, n), lambda i: (i, 0, 0))],
        out_specs=pl.BlockSpec((1, h, n), lambda i: (i, 0, 0)),
        out_shape=jax.ShapeDtypeStruct((b, h, n), F32),
        compiler_params=_cparams(("parallel",)),
        name="logf_cumsum",
    )(lf_t)


def _flash_kernel(q_ref, k_ref, v_ref, ccol_ref, o_ref, cb_ref, m_ref, l_ref, acc_ref):
    hb = pl.program_id(1)
    qi = pl.program_id(2)
    t = ATT_TILE
    n = k_ref.shape[1]

    @pl.when(qi == 0)
    def _():
        cc = ccol_ref[0]
        lane = lax.broadcasted_iota(jnp.int32, cc.shape, 1)
        for a in range(HEADS_PER_STEP):
            col = jnp.sum(jnp.where(lane == hb * HEADS_PER_STEP + a, cc, 0.0),
                          axis=1, keepdims=True)
            cb_ref[a] = jnp.broadcast_to(col, (n, LANES))

    m_ref[...] = jnp.full_like(m_ref, NEG_BIG)
    l_ref[...] = jnp.zeros_like(l_ref)
    acc_ref[...] = jnp.zeros_like(acc_ref)

    def step(ki, masked):
        start = pl.multiple_of(ki * t, t)
        for a in range(HEADS_PER_STEP):
            hl = slice(a * HEAD_DIM, (a + 1) * HEAD_DIM)
            k = k_ref[0, pl.ds(start, t), hl]
            v = v_ref[0, pl.ds(start, t), hl]
            cb = cb_ref[a, pl.ds(start, t), :]
            cb = jnp.concatenate([cb] * (ATT_COLS // LANES), axis=1)
            for j in range(t // ATT_COLS):
                cols = slice(j * ATT_COLS, (j + 1) * ATT_COLS)
                q = q_ref[0, cols, hl]
                s = lax.dot_general(k, q, (((1,), (1,)), ((), ())),
                                    preferred_element_type=F32) - cb
                if masked:
                    r = lax.broadcasted_iota(jnp.int32, s.shape, 0)
                    c = lax.broadcasted_iota(jnp.int32, s.shape, 1) + j * ATT_COLS
                    s = jnp.where(r <= c, s, NEG_BIG)
                m_old = m_ref[a, :, cols]
                m_new = jnp.maximum(m_old, jnp.max(s, axis=0, keepdims=True))
                alpha = jnp.exp2(m_old - m_new)
                p = jnp.exp2(s - m_new)
                l_ref[a, :, cols] = alpha * l_ref[a, :, cols] + jnp.sum(p, axis=0, keepdims=True)
                pv = lax.dot_general(v, p.astype(BF16), (((0,), (0,)), ((), ())),
                                     preferred_element_type=F32)
                acc_ref[a, :, cols] = alpha * acc_ref[a, :, cols] + pv
                m_ref[a, :, cols] = m_new

    def body(ki, carry):
        step(ki, False)
        return carry

    lax.fori_loop(0, qi, body, 0)
    step(qi, True)
    for a in range(HEADS_PER_STEP):
        o_ref[0, :, a * HEAD_DIM:(a + 1) * HEAD_DIM] = (acc_ref[a] / l_ref[a]).T


def _flash_prompt(q, kb, vb, c_col):
    b, n, _ = q.shape
    t = ATT_TILE
    w = HEADS_PER_STEP * HEAD_DIM
    return pl.pallas_call(
        _flash_kernel,
        grid=(b, N_HEADS // HEADS_PER_STEP, n // t),
        in_specs=[
            pl.BlockSpec((1, t, w), lambda bi, h, qi: (bi, qi, h)),
            pl.BlockSpec((1, n, w), lambda bi, h, qi: (bi, 0, h)),
            pl.BlockSpec((1, n, w), lambda bi, h, qi: (bi, 0, h)),
            pl.BlockSpec((1, n, N_HEADS), lambda bi, h, qi: (bi, 0, 0)),
        ],
        out_specs=pl.BlockSpec((1, t, w), lambda bi, h, qi: (bi, qi, h)),
        out_shape=jax.ShapeDtypeStruct((b, n, D_ATTN), F32),
        scratch_shapes=[pltpu.VMEM((HEADS_PER_STEP, n, LANES), F32),
                        pltpu.VMEM((HEADS_PER_STEP, 1, t), F32),
                        pltpu.VMEM((HEADS_PER_STEP, 1, t), F32),
                        pltpu.VMEM((HEADS_PER_STEP, HEAD_DIM, t), F32)],
        compiler_params=_cparams(("arbitrary", "arbitrary", "arbitrary")),
        name="flash_prompt",
    )(q, kb, vb, c_col)


def _page_decay_kernel(lf_ref, o_ref):
    n_pages, heads, page = lf_ref.shape[1:]
    r = lax.broadcasted_iota(jnp.int32, (page, page), 0)
    c = lax.broadcasted_iota(jnp.int32, (page, page), 1)
    later = (r > c).astype(F32)
    x = lf_ref[0].reshape(n_pages * heads, page)
    suffix = jnp.dot(x, later, precision=lax.Precision.HIGHEST, preferred_element_type=F32)
    total = jnp.sum(x, axis=1, keepdims=True)
    o_ref[0, 0] = suffix.reshape(n_pages, heads, page)
    o_ref[0, 1] = jnp.broadcast_to(total, x.shape).reshape(n_pages, heads, page)


def _page_decay(cache_lf_t):
    depth, pool, heads, page = cache_lf_t.shape
    pb = DECAY_PAGES
    return pl.pallas_call(
        _page_decay_kernel,
        grid=(depth, pool // pb),
        in_specs=[pl.BlockSpec((1, pb, heads, page), lambda d, i: (d, i, 0, 0))],
        out_specs=pl.BlockSpec((1, 2, pb, heads, page), lambda d, i: (d, 0, i, 0, 0)),
        out_shape=jax.ShapeDtypeStruct((depth, 2, pool, heads, page), F32),
        compiler_params=_cparams(("parallel", "parallel")),
        name="page_decay",
    )(cache_lf_t)


def _sample_attn_kernel(pt_ref, q_ref, kn_ref, vn_ref, lfc_ref, *rest):
    g = PAGES_PER_SEQ
    k_refs = rest[0:g]
    v_refs = rest[g:2 * g]
    d_refs = rest[2 * g:3 * g]
    o_ref = rest[3 * g]
    rows = q_ref.shape[1]
    page = k_refs[0].shape[2]
    keys = page * N_HEADS
    carry = jnp.zeros((1, keys), F32)

    qb = q_ref[0].astype(BF16)
    row_head = lax.broadcasted_iota(jnp.int32, (rows, keys), 0) % N_HEADS
    key_head = lax.broadcasted_iota(jnp.int32, (rows, keys), 1) % N_HEADS
    head_match = row_head == key_head

    scores = []
    for i in range(g):
        kb = k_refs[i][0, 0].reshape(keys, HEAD_DIM).astype(BF16)
        bias = (d_refs[i][0, 0, 0] + carry) * LOG2E
        carry = carry + d_refs[i][0, 0, 1]
        s = lax.dot_general(qb, kb, (((1,), (1,)), ((), ())), preferred_element_type=F32)
        scores.append(jnp.where(head_match, s + bias, NEG_BIG))
    values = [v_refs[i][0, 0].reshape(keys, HEAD_DIM).astype(BF16) for i in range(g)]

    kn = kn_ref[0].astype(BF16)
    s = lax.dot_general(qb, kn, (((1,), (1,)), ((), ())), preferred_element_type=F32)
    r = lax.broadcasted_iota(jnp.int32, (rows, rows), 0)
    c = lax.broadcasted_iota(jnp.int32, (rows, rows), 1)
    same_head = (r % N_HEADS) == (c % N_HEADS)
    upto = jnp.where(same_head & (r <= c), lfc_ref[0], 0.0)
    cn_row = jnp.sum(upto, axis=0, keepdims=True)
    scores.append(jnp.where(same_head & (c <= r), s - cn_row * LOG2E, NEG_BIG))
    values.append(vn_ref[0].astype(BF16))

    m = jnp.max(scores[0], axis=1, keepdims=True)
    for s in scores[1:]:
        m = jnp.maximum(m, jnp.max(s, axis=1, keepdims=True))
    l = jnp.zeros((rows, 1), F32)
    acc = jnp.zeros((rows, HEAD_DIM), F32)
    for s, v_bf in zip(scores, values):
        p = jnp.exp2(s - m)
        l = l + jnp.sum(p, axis=1, keepdims=True)
        acc = acc + jnp.dot(p.astype(BF16), v_bf, preferred_element_type=F32)
    o_ref[0] = acc / l


def _sample_attn(q, k_new, v_new, lf_new, cache_k, cache_v, decay, page_table, layer):
    n, rows, _ = q.shape
    n_pages = page_table.shape[1]
    page = cache_k.shape[2]
    keys = page * N_HEADS
    g = PAGES_PER_SEQ
    assert n_pages == g

    def page_map5(p):
        return lambda b, pt: (layer, pt[b, n_pages - 1 - p], 0, 0, 0)

    per_seq = lambda b, pt: (b, 0, 0)
    in_specs = [
        pl.BlockSpec((1, rows, HEAD_DIM), per_seq),
        pl.BlockSpec((1, rows, HEAD_DIM), per_seq),
        pl.BlockSpec((1, rows, HEAD_DIM), per_seq),
        pl.BlockSpec((1, rows, 1), per_seq),
    ]
    in_specs += [pl.BlockSpec((1, 1, page, N_HEADS, HEAD_DIM), page_map5(p)) for p in range(g)]
    in_specs += [pl.BlockSpec((1, 1, page, N_HEADS, HEAD_DIM), page_map5(p)) for p in range(g)]
    in_specs += [pl.BlockSpec((1, 1, 2, 1, keys), page_map5(p)) for p in range(g)]
    grid_spec = pltpu.PrefetchScalarGridSpec(
        num_scalar_prefetch=1,
        grid=(n,),
        in_specs=in_specs,
        out_specs=pl.BlockSpec((1, rows, HEAD_DIM), per_seq),
    )
    return pl.pallas_call(
        _sample_attn_kernel,
        grid_spec=grid_spec,
        out_shape=jax.ShapeDtypeStruct((n, rows, HEAD_DIM), F32),
        compiler_params=_cparams(("parallel",)),
        name="sample_attn",
    )(page_table, q, k_new, v_new, lf_new.reshape(n, rows, 1),
      *([cache_k] * g), *([cache_v] * g), *([decay] * g))


def _discretise(a_re, a_im, log_dt):
    dt = jnp.exp(log_dt)
    mag = jnp.exp(dt * a_re)
    ab_re = mag * jnp.cos(dt * a_im)
    ab_im = mag * jnp.sin(dt * a_im)
    den = a_re * a_re + a_im * a_im
    f_re = ((ab_re - 1.0) * a_re + ab_im * a_im) / den
    f_im = (ab_im * a_re - (ab_re - 1.0) * a_im) / den
    return ab_re, ab_im, f_re, f_im


def _s5_prep_kernel(pe_ref, b_ref, pf_ref, c_ref, ps_ref,
                    e_ref, f_ref, m_ref, pw_ref, e32_ref, f0_ref):
    t_len = S5_CHUNK
    sc = STATE_COLS
    ab_re, ab_im, f_re, f_im = _discretise(pe_ref[0, 0], pe_ref[0, 1], pe_ref[0, 2])
    b_re = b_ref[0, 0]
    b_im = b_ref[0, 1]
    row_g = lax.broadcasted_iota(jnp.int32, (LANES, sc), 0) // SSM_GROUP
    col_g = lax.broadcasted_iota(jnp.int32, (LANES, sc), 1) // SSM_STATE
    same = row_g == col_g
    w_re = jnp.where(same, f_re * b_re - f_im * b_im, 0.0)
    w_im = jnp.where(same, f_re * b_im + f_im * b_re, 0.0)
    for j in range(t_len):
        s = t_len - 1 - j
        e32_ref[s * LANES:(s + 1) * LANES, 0:sc] = w_re
        e32_ref[s * LANES:(s + 1) * LANES, sc:2 * sc] = w_im
        w_re, w_im = ab_re * w_re - ab_im * w_im, ab_re * w_im + ab_im * w_re
    e_ref[0] = e32_ref[...].astype(BF16)

    ab_re, ab_im, _, _ = _discretise(pf_ref[0, 0], pf_ref[0, 1], pf_ref[0, 2])
    row_g = lax.broadcasted_iota(jnp.int32, (sc, LANES), 0) // SSM_STATE
    col_g = lax.broadcasted_iota(jnp.int32, (sc, LANES), 1) // SSM_GROUP
    same = row_g == col_g
    g_re = jnp.where(same, c_ref[0, 0], 0.0)
    g_im = jnp.where(same, c_ref[0, 1], 0.0)
    f0_ref[0:sc, :] = g_re
    f0_ref[sc:2 * sc, :] = -g_im
    for t in range(t_len):
        g_re, g_im = g_re * ab_re - g_im * ab_im, g_re * ab_im + g_im * ab_re
        f_ref[0, 0:sc, t * LANES:(t + 1) * LANES] = g_re.astype(BF16)
        f_ref[0, sc:2 * sc, t * LANES:(t + 1) * LANES] = (-g_im).astype(BF16)

    lag = jnp.dot(e32_ref[...], f0_ref[...], precision=lax.Precision.HIGHEST,
                  preferred_element_type=F32).astype(BF16)
    m_ref[0] = jnp.zeros(m_ref.shape[1:], BF16)
    for t in range(t_len):
        m_ref[0, 0:(t + 1) * LANES, t * LANES:(t + 1) * LANES] = lag[(t_len - 1 - t) * LANES:, :]

    a_re, a_im, _, _ = _discretise(ps_ref[0, 0:1], ps_ref[0, 1:2], ps_ref[0, 2:3])
    a2_re, a2_im = a_re * a_re - a_im * a_im, 2.0 * a_re * a_im
    a4_re, a4_im = a2_re * a2_re - a2_im * a2_im, 2.0 * a2_re * a2_im
    a8_re, a8_im = a4_re * a4_re - a4_im * a4_im, 2.0 * a4_re * a4_im
    a16_re, a16_im = a8_re * a8_re - a8_im * a8_im, 2.0 * a8_re * a8_im
    pw_ref[0, 0:1, 0:sc] = a4_re
    pw_ref[0, 0:1, sc:2 * sc] = a4_im
    pw_ref[0, 1:2, 0:sc] = a16_re
    pw_ref[0, 1:2, sc:2 * sc] = a16_im


def _s5_prep(a_re, a_im, log_dt, b_re, b_im, c_re, c_im):
    nb, gb, st, ch = N_LANE_BLOCKS, GROUPS_PER_BLOCK, SSM_STATE, SSM_GROUP
    ldt = jnp.broadcast_to(log_dt[:, None], (SSM_GROUPS, st))
    par = jnp.stack([a_re, a_im, ldt], axis=0).reshape(3, nb, gb, st)
    pe = jnp.broadcast_to(par[:, :, :, None, None, :], (3, nb, gb, ch, gb, st))
    pe = pe.reshape(3, nb, LANES, STATE_COLS).transpose(1, 0, 2, 3)
    bb = jnp.stack([b_re, b_im], axis=0).reshape(2, nb, gb, st, ch).transpose(0, 1, 2, 4, 3)
    bb = jnp.broadcast_to(bb[:, :, :, :, None, :], (2, nb, gb, ch, gb, st))
    bb = bb.reshape(2, nb, LANES, STATE_COLS).transpose(1, 0, 2, 3)
    pf = jnp.broadcast_to(par[:, :, :, :, None, None], (3, nb, gb, st, gb, ch))
    pf = pf.reshape(3, nb, STATE_COLS, LANES).transpose(1, 0, 2, 3)
    cc = jnp.stack([c_re, c_im], axis=0).reshape(2, nb, gb, ch, st).transpose(0, 1, 2, 4, 3)
    cc = jnp.broadcast_to(cc[:, :, :, :, None, :], (2, nb, gb, st, gb, ch))
    cc = cc.reshape(2, nb, STATE_COLS, LANES).transpose(1, 0, 2, 3)
    ps = par.reshape(3, nb, STATE_COLS).transpose(1, 0, 2)

    tl = S5_CHUNK * LANES
    blk4 = lambda i: (i, 0, 0, 0)
    blk3 = lambda i: (i, 0, 0)
    return pl.pallas_call(
        _s5_prep_kernel,
        grid=(nb,),
        in_specs=[
            pl.BlockSpec((1, 3, LANES, STATE_COLS), blk4),
            pl.BlockSpec((1, 2, LANES, STATE_COLS), blk4),
            pl.BlockSpec((1, 3, STATE_COLS, LANES), blk4),
            pl.BlockSpec((1, 2, STATE_COLS, LANES), blk4),
            pl.BlockSpec((1, 3, STATE_COLS), blk3),
        ],
        out_specs=(
            pl.BlockSpec((1, tl, 2 * STATE_COLS), blk3),
            pl.BlockSpec((1, 2 * STATE_COLS, tl), blk3),
            pl.BlockSpec((1, tl, tl), blk3),
            pl.BlockSpec((1, 2, 2 * STATE_COLS), blk3),
        ),
        out_shape=(
            jax.ShapeDtypeStruct((nb, tl, 2 * STATE_COLS), BF16),
            jax.ShapeDtypeStruct((nb, 2 * STATE_COLS, tl), BF16),
            jax.ShapeDtypeStruct((nb, tl, tl), BF16),
            jax.ShapeDtypeStruct((nb, 2, 2 * STATE_COLS), F32),
        ),
        scratch_shapes=[pltpu.VMEM((tl, 2 * STATE_COLS), F32),
                        pltpu.VMEM((2 * STATE_COLS, LANES), F32)],
        compiler_params=_cparams(("parallel",)),
        name="s5_prep",
    )(pe, bb, pf, cc, ps)


def _s5_prompt_kernel(u_ref, e_ref, m_ref, f_ref, pw_ref, d_ref,
                      y_ref, hl_ref, ub_ref, sloc_ref, hp_ref, hpb_ref):
    ct = pl.program_id(1)
    n_rows = u_ref.shape[1]
    n_batch = hl_ref.shape[1]
    per_batch = n_rows // n_batch
    sc = STATE_COLS

    @pl.when(ct == 0)
    def _():
        ub = u_ref[0].astype(BF16)
        ub_ref[...] = ub
        sloc_ref[...] = jnp.dot(ub, e_ref[0], preferred_element_type=F32)
        p_re = pw_ref[0, 1:2, 0:sc]
        p_im = pw_ref[0, 1:2, sc:2 * sc]

        def body(k, carry):
            new = []
            for b in range(n_batch):
                h_re, h_im = carry[2 * b], carry[2 * b + 1]
                row = b * per_batch + k
                hp_ref[pl.ds(row, 1), 0:sc] = h_re
                hp_ref[pl.ds(row, 1), sc:2 * sc] = h_im
                s_re = sloc_ref[pl.ds(row, 1), 0:sc]
                s_im = sloc_ref[pl.ds(row, 1), sc:2 * sc]
                new.append(p_re * h_re - p_im * h_im + s_re)
                new.append(p_re * h_im + p_im * h_re + s_im)
            return tuple(new)

        zero = jnp.zeros((1, sc), F32)
        final = lax.fori_loop(0, per_batch, body, (zero,) * (2 * n_batch))
        for b in range(n_batch):
            hl_ref[0, b:b + 1, 0:sc] = final[2 * b]
            hl_ref[0, b:b + 1, sc:2 * sc] = final[2 * b + 1]
        hpb_ref[...] = hp_ref[...].astype(BF16)

    w = y_ref.shape[2]
    col = pl.multiple_of(ct * w, w)
    y = jnp.dot(ub_ref[...], m_ref[0], preferred_element_type=F32)
    y = y + jnp.dot(hpb_ref[...], f_ref[0], preferred_element_type=F32)
    y_ref[0] = y + d_ref[0] * u_ref[0, :, pl.ds(col, w)]


def _s5_prompt(u_chunks, e_op, m_op, f_op, pw, d_tiled, n_batch):
    nb, n_rows, tl = u_chunks.shape
    w = 512
    return pl.pallas_call(
        _s5_prompt_kernel,
        grid=(nb, tl // w),
        in_specs=[
            pl.BlockSpec((1, n_rows, tl), lambda i, c: (i, 0, 0)),
            pl.BlockSpec((1, tl, 2 * STATE_COLS), lambda i, c: (i, 0, 0)),
            pl.BlockSpec((1, tl, w), lambda i, c: (i, 0, c)),
            pl.BlockSpec((1, 2 * STATE_COLS, w), lambda i, c: (i, 0, c)),
            pl.BlockSpec((1, 2, 2 * STATE_COLS), lambda i, c: (i, 0, 0)),
            pl.BlockSpec((1, 1, w), lambda i, c: (i, 0, 0)),
        ],
        out_specs=(
            pl.BlockSpec((1, n_rows, w), lambda i, c: (i, 0, c)),
            pl.BlockSpec((1, n_batch, 2 * STATE_COLS), lambda i, c: (i, 0, 0)),
        ),
        out_shape=(
            jax.ShapeDtypeStruct((nb, n_rows, tl), F32),
            jax.ShapeDtypeStruct((nb, n_batch, 2 * STATE_COLS), F32),
        ),
        scratch_shapes=[pltpu.VMEM((n_rows, tl), BF16),
                        pltpu.VMEM((n_rows, 2 * STATE_COLS), F32),
                        pltpu.VMEM((n_rows, 2 * STATE_COLS), F32),
                        pltpu.VMEM((n_rows, 2 * STATE_COLS), BF16)],
        compiler_params=_cparams(("parallel", "arbitrary")),
        name="s5_prompt",
    )(u_chunks, e_op, m_op, f_op, pw, d_tiled)


def _s5_sample_kernel(u_ref, e_ref, m_ref, f_ref, pw_ref, d_ref, h0_ref, y_ref, hl_ref):
    sc = STATE_COLS
    u = u_ref[0]
    ub = u.astype(BF16)
    h0 = h0_ref[0]
    y = jnp.dot(ub, m_ref[0], preferred_element_type=F32)
    y = y + jnp.dot(h0.astype(BF16), f_ref[0], preferred_element_type=F32)
    y_ref[0] = y + d_ref[0] * u
    sloc = jnp.dot(ub, e_ref[0], preferred_element_type=F32)
    p_re = pw_ref[0, 0:1, 0:sc]
    p_im = pw_ref[0, 0:1, sc:2 * sc]
    h_re = h0[:, 0:sc]
    h_im = h0[:, sc:2 * sc]
    hl_ref[0, :, 0:sc] = p_re * h_re - p_im * h_im + sloc[:, 0:sc]
    hl_ref[0, :, sc:2 * sc] = p_re * h_im + p_im * h_re + sloc[:, sc:2 * sc]


def _s5_sample(u_chunks, e_op, m_op, f_op, pw, d_tiled, h0):
    nb, n_seq, w = u_chunks.shape
    tl = e_op.shape[1]
    e_blk = (tl - w) // w
    return pl.pallas_call(
        _s5_sample_kernel,
        grid=(nb,),
        in_specs=[
            pl.BlockSpec((1, n_seq, w), lambda i: (i, 0, 0)),
            pl.BlockSpec((1, w, 2 * STATE_COLS), lambda i: (i, e_blk, 0)),
            pl.BlockSpec((1, w, w), lambda i: (i, 0, 0)),
            pl.BlockSpec((1, 2 * STATE_COLS, w), lambda i: (i, 0, 0)),
            pl.BlockSpec((1, 2, 2 * STATE_COLS), lambda i: (i, 0, 0)),
            pl.BlockSpec((1, 1, w), lambda i: (i, 0, 0)),
            pl.BlockSpec((1, n_seq, 2 * STATE_COLS), lambda i: (i, 0, 0)),
        ],
        out_specs=(
            pl.BlockSpec((1, n_seq, w), lambda i: (i, 0, 0)),
            pl.BlockSpec((1, n_seq, 2 * STATE_COLS), lambda i: (i, 0, 0)),
        ),
        out_shape=(
            jax.ShapeDtypeStruct((nb, n_seq, w), F32),
            jax.ShapeDtypeStruct((nb, n_seq, 2 * STATE_COLS), F32),
        ),
        compiler_params=_cparams(("parallel",)),
        name="s5_sample",
    )(u_chunks, e_op, m_op, f_op, pw, d_tiled, h0)


def _mix_out_kernel(x_ref, attn_ref, y_ref, gw_ref, gb_ref, ga_ref, gs_ref, wo_ref, gp_ref,
                    o_ref, ycat_ref, mg_ref):
    for lb in range(N_LANE_BLOCKS):
        ycat_ref[:, lb * LANES:(lb + 1) * LANES] = jax.nn.gelu(y_ref[lb])
    y = ycat_ref[...]
    z = jnp.dot(y.astype(BF16), gw_ref[...], preferred_element_type=F32) + gb_ref[...]
    s = y * jax.nn.sigmoid(z)
    mg_ref[:, 0:D_ATTN] = _rms(attn_ref[...], ga_ref[...]).astype(BF16)
    mg_ref[:, D_ATTN:] = _rms(s, gs_ref[...]).astype(BF16)
    m = jnp.dot(mg_ref[...], wo_ref[...], preferred_element_type=F32)
    o_ref[...] = x_ref[...] + _rms(m, gp_ref[...])


def _mix_out(x, attn, y_blk, glu_w, glu_b, g_attn, g_ssm, w_out, g_post, layer):
    m = x.shape[0]
    tm = 256
    row = lambda i: (i, 0)
    const = lambda i: (0, 0)
    per_layer = lambda i: (layer, 0, 0)
    return pl.pallas_call(
        _mix_out_kernel,
        grid=(m // tm,),
        in_specs=[
            pl.BlockSpec((tm, D_MODEL), row),
            pl.BlockSpec((tm, D_ATTN), row),
            pl.BlockSpec((N_LANE_BLOCKS, tm, LANES), lambda i: (0, i, 0)),
            pl.BlockSpec((None, D_SSM, D_SSM), per_layer),
            pl.BlockSpec((1, D_SSM), const),
            pl.BlockSpec((1, D_ATTN), const),
            pl.BlockSpec((1, D_SSM), const),
            pl.BlockSpec((None, D_MODEL, D_MODEL), per_layer),
            pl.BlockSpec((1, D_MODEL), const),
        ],
        out_specs=pl.BlockSpec((tm, D_MODEL), row),
        out_shape=jax.ShapeDtypeStruct((m, D_MODEL), F32),
        scratch_shapes=[pltpu.VMEM((tm, D_SSM), F32), pltpu.VMEM((tm, D_MODEL), BF16)],
        compiler_params=_cparams(("parallel",)),
        name="mix_out",
    )(x, attn, y_blk, glu_w, glu_b, g_attn, g_ssm, w_out, g_post)


def _state_to_blocks(h):
    n = h.shape[0]
    h = h.reshape(n, N_LANE_BLOCKS, STATE_COLS, 2)
    return h.transpose(1, 0, 3, 2).reshape(N_LANE_BLOCKS, n, 2 * STATE_COLS)


def _blocks_to_state(h):
    n = h.shape[1]
    h = h.reshape(N_LANE_BLOCKS, n, 2, STATE_COLS).transpose(1, 0, 3, 2)
    return h.reshape(n, SSM_GROUPS, SSM_STATE, 2)


def _project_in(x, l, w):
    u_blk, q, k, v, kb, vb, lf = _mix_in(x, w["ng"][l][2], w["w_in"], w["w_f"], w["b_f"], l)
    return u_blk, q, k, v, kb, vb, lf[:, :N_HEADS]


def _project_out(x, attn, y_blk, l, w):
    return _mix_out(x, attn, y_blk, w["glu_w"], w["glu_b"][l], w["g_attn"][l], w["g_ssm"][l],
                    w["w_out"], w["ng"][l][3], l)


def kernel(x_prompt, x_sample, cache_k, cache_v, cache_logf, state_ssm, page_table, norm_g, ffn_gate, ffn_up, ffn_down, w_in, b_forget, ssm_a_re, ssm_a_im, ssm_log_dt, ssm_b_re, ssm_b_im, ssm_c_re, ssm_c_im, ssm_d, glu_w, glu_b, g_attn_out, g_ssm_out, w_out):
    batch, seq, _ = x_prompt.shape
    n_seq, n_new, _ = x_sample.shape
    depth = norm_g.shape[0]
    n_pool, page = cache_k.shape[1], cache_k.shape[2]
    assert seq % ROW_TILE == 0 and seq % ATT_TILE == 0 and seq % S5_CHUNK == 0
    assert (n_seq * n_new) % 256 == 0 and page_table.shape[1] == PAGES_PER_SEQ
    assert n_pool % DECAY_PAGES == 0 and N_HEADS == SUBLANES

    xp = x_prompt.reshape(batch * seq, D_MODEL)
    xs = x_sample.reshape(n_seq * n_new, D_MODEL)

    n_main = D_SSM + 3 * D_ATTN
    w = {
        "ng": [[norm_g[l, i].reshape(1, D_MODEL) for i in range(6)] for l in range(depth)],
        "wg": ffn_gate.astype(BF16),
        "wu": ffn_up.astype(BF16),
        "wd": ffn_down.astype(BF16),
        "w_in": w_in.astype(BF16),
        "w_f": jnp.pad(w_in[:, :, n_main:], ((0, 0), (0, 0), (0, LANES - N_HEADS))).astype(BF16),
        "b_f": jnp.pad(b_forget, ((0, 0), (0, LANES - N_HEADS))).reshape(depth, 1, LANES),
        "glu_w": glu_w.astype(BF16),
        "glu_b": [glu_b[l].reshape(1, D_SSM) for l in range(depth)],
        "g_attn": [g_attn_out[l].reshape(1, D_ATTN) for l in range(depth)],
        "g_ssm": [g_ssm_out[l].reshape(1, D_SSM) for l in range(depth)],
        "w_out": w_out.astype(BF16),
    }

    decay = _page_decay(cache_logf.transpose(0, 1, 3, 2))
    decay = decay.transpose(0, 2, 1, 4, 3).reshape(depth, n_pool, 2, 1, page * N_HEADS)

    outs = {name: [] for name in ("kp", "vp", "lfp", "sp", "ks", "vs", "lfs", "ss")}
    for l in range(depth):
        e_op, f_op, m_op, pw = _s5_prep(ssm_a_re[l], ssm_a_im[l], ssm_log_dt[l],
                                        ssm_b_re[l], ssm_b_im[l], ssm_c_re[l], ssm_c_im[l])
        d_blk = ssm_d[l].reshape(N_LANE_BLOCKS, 1, LANES)
        d_prompt = jnp.tile(d_blk, (1, 1, 512 // LANES))
        d_sample = jnp.tile(d_blk, (1, 1, n_new))

        ng = w["ng"][l]
        ffn_w = (w["wg"], w["wu"], w["wd"])
        rows = n_new * N_HEADS

        xp = _ffn_half(xp, ng[0], ng[1], *ffn_w, l, 0)
        u1, q1, k1, v1, kb1, vb1, lf1 = _project_in(xp, l, w)
        lf_t = lf1.reshape(batch, seq, N_HEADS).transpose(0, 2, 1)
        c_col = _cumsum_heads(lf_t).transpose(0, 2, 1)
        attn1 = _flash_prompt(q1.reshape(batch, seq, D_ATTN), kb1.reshape(batch, seq, D_ATTN),
                              vb1.reshape(batch, seq, D_ATTN), c_col)
        u_chunks = u1.reshape(N_LANE_BLOCKS, batch * seq // S5_CHUNK, S5_CHUNK * LANES)
        y1, h1 = _s5_prompt(u_chunks, e_op, m_op, f_op, pw, d_prompt, batch)
        s1 = _blocks_to_state(h1)
        xp = _project_out(xp, attn1.reshape(batch * seq, D_ATTN),
                          y1.reshape(N_LANE_BLOCKS, batch * seq, LANES), l, w)
        xp = _ffn_half(xp, ng[4], ng[5], *ffn_w, l, 1)

        xs = _ffn_half(xs, ng[0], ng[1], *ffn_w, l, 0)
        u2, q2, k2, v2, _, _, lf2 = _project_in(xs, l, w)
        attn2 = _sample_attn(q2.astype(F32).reshape(n_seq, rows, HEAD_DIM),
                             k2.reshape(n_seq, rows, HEAD_DIM), v2.reshape(n_seq, rows, HEAD_DIM),
                             lf2.reshape(n_seq, rows), cache_k, cache_v, decay, page_table, l)
        attn2 = attn2.reshape(n_seq * n_new, D_ATTN)
        y2, h2 = _s5_sample(u2.reshape(N_LANE_BLOCKS, n_seq, n_new * LANES), e_op, m_op, f_op, pw,
                            d_sample, _state_to_blocks(state_ssm[l]))
        s2 = _blocks_to_state(h2)
        xs = _project_out(xs, attn2, y2.reshape(N_LANE_BLOCKS, n_seq * n_new, LANES), l, w)
        xs = _ffn_half(xs, ng[4], ng[5], *ffn_w, l, 1)
        outs["kp"].append(k1.reshape(batch, seq, N_HEADS, HEAD_DIM))
        outs["vp"].append(v1.reshape(batch, seq, N_HEADS, HEAD_DIM))
        outs["lfp"].append(lf1.reshape(batch, seq, N_HEADS))
        outs["sp"].append(s1)
        outs["ks"].append(k2.reshape(n_seq, n_new, N_HEADS, HEAD_DIM))
        outs["vs"].append(v2.reshape(n_seq, n_new, N_HEADS, HEAD_DIM))
        outs["lfs"].append(lf2.reshape(n_seq, n_new, N_HEADS))
        outs["ss"].append(s2)

    return (xp.reshape(batch, seq, D_MODEL), xs.reshape(n_seq, n_new, D_MODEL),
            jnp.stack(outs["kp"]), jnp.stack(outs["vp"]), jnp.stack(outs["lfp"]),
            jnp.stack(outs["sp"]),
            jnp.stack(outs["ks"]), jnp.stack(outs["vs"]), jnp.stack(outs["lfs"]),
            jnp.stack(outs["ss"]))
```

```python
import math

import jax
import jax.numpy as jnp
from jax import lax
from jax.experimental import pallas as pl
from jax.experimental.pallas import tpu as pltpu

F32 = jnp.float32
BF16 = jnp.bfloat16

D_MODEL = 2048
D_SSM = 1024
D_ATTN = 1024
HEAD_DIM = 128
N_HEADS = 8
SSM_GROUP = 16
SSM_GROUPS = 64
SSM_STATE = 64
D_FF = 5504
EPS = 1e-6
LOG2E = math.log2(math.e)

LANES = 128
SUBLANES = 8
GROUPS_PER_BLOCK = LANES // SSM_GROUP
N_LANE_BLOCKS = D_SSM // LANES
STATE_COLS = GROUPS_PER_BLOCK * SSM_STATE
S5_CHUNK = 16
DECODE_CHUNK = 4
S5_COL_TILE = 512
MIX_OUT_ROWS = 256
FF_TILE = 1024
FF_STEPS = (D_FF + FF_TILE - 1) // FF_TILE
FF_TAIL = D_FF - (FF_STEPS - 1) * FF_TILE
ROW_TILE = 512
ATT_TILE = 512
ATT_COLS = 512
HEADS_PER_STEP = 2
PAGES_PER_SEQ = 16
DECAY_PAGES = 256
VMEM_LIMIT = 56 * 1024 * 1024
NEG_BIG = -1e30


def _cparams(sem):
    return pltpu.CompilerParams(dimension_semantics=sem, vmem_limit_bytes=VMEM_LIMIT)


def _rms(x, g):
    ms = jnp.mean(x * x, axis=-1, keepdims=True)
    return x * lax.rsqrt(ms + EPS) * g


def _log_sigmoid(z):
    return jnp.minimum(z, 0.0) - jnp.log1p(jnp.exp(-jnp.abs(z)))


def _ffn_kernel(x_ref, gpre_ref, gpost_ref, wg_ref, wu_ref, wd_ref, o_ref, h_ref):
    j = pl.program_id(1)
    last = pl.num_programs(1) - 1
    acc_ref = o_ref

    @pl.when(j == 0)
    def _():
        h_ref[...] = _rms(x_ref[...], gpre_ref[...]).astype(BF16)
        acc_ref[...] = jnp.zeros_like(acc_ref)

    def hidden_tile(width):
        h = h_ref[...]
        g = jnp.dot(h, wg_ref[:, 0:width], preferred_element_type=F32)
        u = jnp.dot(h, wu_ref[:, 0:width], preferred_element_type=F32)
        a = (g * jax.nn.sigmoid(g) * u).astype(BF16)
        acc_ref[...] += jnp.dot(a, wd_ref[0:width, :], preferred_element_type=F32)

    if FF_TAIL == FF_TILE:
        hidden_tile(FF_TILE)
    else:
        pl.when(j < last)(lambda: hidden_tile(FF_TILE))
        pl.when(j == last)(lambda: hidden_tile(FF_TAIL))

    @pl.when(j == last)
    def _():
        o_ref[...] = x_ref[...] + 0.5 * _rms(acc_ref[...], gpost_ref[...])


def _ffn_half(x, g_pre, g_post, wg, wu, wd, layer, half):
    m = x.shape[0]
    tm = min(ROW_TILE, m)
    return pl.pallas_call(
        _ffn_kernel,
        grid=(m // tm, FF_STEPS),
        in_specs=[
            pl.BlockSpec((tm, D_MODEL), lambda i, j: (i, 0)),
            pl.BlockSpec((1, D_MODEL), lambda i, j: (0, 0)),
            pl.BlockSpec((1, D_MODEL), lambda i, j: (0, 0)),
            pl.BlockSpec((None, None, D_MODEL, FF_TILE), lambda i, j: (layer, half, 0, j)),
            pl.BlockSpec((None, None, D_MODEL, FF_TILE), lambda i, j: (layer, half, 0, j)),
            pl.BlockSpec((None, None, FF_TILE, D_MODEL), lambda i, j: (layer, half, j, 0)),
        ],
        out_specs=pl.BlockSpec((tm, D_MODEL), lambda i, j: (i, 0)),
        out_shape=jax.ShapeDtypeStruct((m, D_MODEL), F32),
        scratch_shapes=[pltpu.VMEM((tm, D_MODEL), BF16)],
        compiler_params=_cparams(("parallel", "arbitrary")),
        name="ffn_half",
    )(x, g_pre, g_post, wg, wu, wd)


def _mix_in_kernel(x_ref, g_ref, w_ref, wf_ref, bf_ref,
                   u_ref, q_ref, k_ref, v_ref, kb_ref, vb_ref, lf_ref, h_ref):
    j = pl.program_id(1)

    @pl.when(j == 0)
    def _():
        h = _rms(x_ref[...], g_ref[...]).astype(BF16)
        h_ref[...] = h
        z = jnp.dot(h, wf_ref[...], preferred_element_type=F32) + bf_ref[...]
        lf_ref[...] = _log_sigmoid(z)

    p = jnp.dot(h_ref[...], w_ref[...], preferred_element_type=F32)

    @pl.when(j == 0)
    def _():
        for lb in range(N_LANE_BLOCKS):
            u_ref[lb] = p[:, lb * LANES:(lb + 1) * LANES]

    @pl.when(j == 1)
    def _():
        q_ref[...] = (p * (HEAD_DIM ** -0.5 * LOG2E)).astype(BF16)

    @pl.when(j == 2)
    def _():
        k_ref[...] = p
        kb_ref[...] = p.astype(BF16)

    @pl.when(j == 3)
    def _():
        v_ref[...] = p
        vb_ref[...] = p.astype(BF16)


def _mix_in(x, g, w_main, w_f, b_f, layer):
    m = x.shape[0]
    tm = min(ROW_TILE, m)
    row = lambda i, j: (i, 0)
    out_shape = (
        jax.ShapeDtypeStruct((N_LANE_BLOCKS, m, LANES), F32),
        jax.ShapeDtypeStruct((m, D_ATTN), BF16),
        jax.ShapeDtypeStruct((m, D_ATTN), F32),
        jax.ShapeDtypeStruct((m, D_ATTN), F32),
        jax.ShapeDtypeStruct((m, D_ATTN), BF16),
        jax.ShapeDtypeStruct((m, D_ATTN), BF16),
        jax.ShapeDtypeStruct((m, LANES), F32),
    )
    out_specs = (
        pl.BlockSpec((N_LANE_BLOCKS, tm, LANES), lambda i, j: (0, i, 0)),
        pl.BlockSpec((tm, D_ATTN), row),
        pl.BlockSpec((tm, D_ATTN), row),
        pl.BlockSpec((tm, D_ATTN), row),
        pl.BlockSpec((tm, D_ATTN), row),
        pl.BlockSpec((tm, D_ATTN), row),
        pl.BlockSpec((tm, LANES), row),
    )
    return pl.pallas_call(
        _mix_in_kernel,
        grid=(m // tm, 4),
        in_specs=[
            pl.BlockSpec((tm, D_MODEL), row),
            pl.BlockSpec((1, D_MODEL), lambda i, j: (0, 0)),
            pl.BlockSpec((None, D_MODEL, D_ATTN), lambda i, j: (layer, 0, j)),
            pl.BlockSpec((None, D_MODEL, LANES), lambda i, j: (layer, 0, 0)),
            pl.BlockSpec((None, 1, LANES), lambda i, j: (layer, 0, 0)),
        ],
        out_specs=out_specs,
        out_shape=out_shape,
        scratch_shapes=[pltpu.VMEM((tm, D_MODEL), BF16)],
        compiler_params=_cparams(("parallel", "arbitrary")),
        name="mix_in",
    )(x, g, w_main, w_f, b_f)


def _cumsum_kernel(x_ref, o_ref):
    blk = 512
    n = x_ref.shape[2]
    r = lax.broadcasted_iota(jnp.int32, (blk, blk), 0)
    c = lax.broadcasted_iota(jnp.int32, (blk, blk), 1)
    tri = (r <= c).astype(F32)
    carry = jnp.zeros((N_HEADS, 1), F32)
    for b in range(n // blk):
        xb = x_ref[0, :, b * blk:(b + 1) * blk]
        cb = jnp.dot(xb, tri, precision=lax.Precision.HIGHEST,
                     preferred_element_type=F32) + carry
        o_ref[0, :, b * blk:(b + 1) * blk] = cb * LOG2E
        carry = cb[:, blk - 1:blk]


def _cumsum_heads(lf_t):
    b, h, n = lf_t.shape
    return pl.pallas_call(
        _cumsum_kernel,
        grid=(b,),
        in_specs=[pl.BlockSpec((1, h, n), lambda i: (i, 0, 0))],
        out_specs=pl.BlockSpec((1, h, n), lambda i: (i, 0, 0)),
        out_shape=jax.ShapeDtypeStruct((b, h, n), F32),
        compiler_params=_cparams(("parallel",)),
        name="logf_cumsum",
    )(lf_t)


def _flash_kernel(q_ref, k_ref, v_ref, ccol_ref, o_ref, cb_ref, m_ref, l_ref, acc_ref):
    hb = pl.program_id(1)
    qi = pl.program_id(2)
    t = ATT_TILE
    n = k_ref.shape[1]

    @pl.when(qi == 0)
    def _():
        cc = ccol_ref[0]
        lane = lax.broadcasted_iota(jnp.int32, cc.shape, 1)
        for a in range(HEADS_PER_STEP):
            col = jnp.sum(jnp.where(lane == hb * HEADS_PER_STEP + a, cc, 0.0),
                          axis=1, keepdims=True)
            cb_ref[a] = jnp.broadcast_to(col, (n, LANES))

    m_ref[...] = jnp.full_like(m_ref, NEG_BIG)
    l_ref[...] = jnp.zeros_like(l_ref)
    acc_ref[...] = jnp.zeros_like(acc_ref)

    def step(ki, masked):
        start = pl.multiple_of(ki * t, t)
        for a in range(HEADS_PER_STEP):
            hl = slice(a * HEAD_DIM, (a + 1) * HEAD_DIM)
            k = k_ref[0, pl.ds(start, t), hl]
            v = v_ref[0, pl.ds(start, t), hl]
            cb = cb_ref[a, pl.ds(start, t), :]
            cb = jnp.concatenate([cb] * (ATT_COLS // LANES), axis=1)
            for j in range(t // ATT_COLS):
                cols = slice(j * ATT_COLS, (j + 1) * ATT_COLS)
                q = q_ref[0, cols, hl]
                s = lax.dot_general(k, q, (((1,), (1,)), ((), ())),
                                    preferred_element_type=F32) - cb
                if masked:
                    r = lax.broadcasted_iota(jnp.int32, s.shape, 0)
                    c = lax.broadcasted_iota(jnp.int32, s.shape, 1) + j * ATT_COLS
                    s = jnp.where(r <= c, s, NEG_BIG)
                m_old = m_ref[a, :, cols]
                m_new = jnp.maximum(m_old, jnp.max(s, axis=0, keepdims=True))
                alpha = jnp.exp2(m_old - m_new)
                p = jnp.exp2(s - m_new)
                l_ref[a, :, cols] = alpha * l_ref[a, :, cols] + jnp.sum(p, axis=0, keepdims=True)
                pv = lax.dot_general(v, p.astype(BF16), (((0,), (0,)), ((), ())),
                                     preferred_element_type=F32)
                acc_ref[a, :, cols] = alpha * acc_ref[a, :, cols] + pv
                m_ref[a, :, cols] = m_new

    def body(ki, carry):
        step(ki, False)
        return carry

    lax.fori_loop(0, qi, body, 0)
    step(qi, True)
    for a in range(HEADS_PER_STEP):
        o_ref[0, :, a * HEAD_DIM:(a + 1) * HEAD_DIM] = (acc_ref[a] / l_ref[a]).T


def _flash_prompt(q, kb, vb, c_col):
    b, n, _ = q.shape
    t = ATT_TILE
    w = HEADS_PER_STEP * HEAD_DIM
    return pl.pallas_call(
        _flash_kernel,
        grid=(b, N_HEADS // HEADS_PER_STEP, n // t),
        in_specs=[
            pl.BlockSpec((1, t, w), lambda bi, h, qi: (bi, qi, h)),
            pl.BlockSpec((1, n, w), lambda bi, h, qi: (bi, 0, h)),
            pl.BlockSpec((1, n, w), lambda bi, h, qi: (bi, 0, h)),
            pl.BlockSpec((1, n, N_HEADS), lambda bi, h, qi: (bi, 0, 0)),
        ],
        out_specs=pl.BlockSpec((1, t, w), lambda bi, h, qi: (bi, qi, h)),
        out_shape=jax.ShapeDtypeStruct((b, n, D_ATTN), F32),
        scratch_shapes=[pltpu.VMEM((HEADS_PER_STEP, n, LANES), F32),
                        pltpu.VMEM((HEADS_PER_STEP, 1, t), F32),
                        pltpu.VMEM((HEADS_PER_STEP, 1, t), F32),
                        pltpu.VMEM((HEADS_PER_STEP, HEAD_DIM, t), F32)],
        compiler_params=_cparams(("arbitrary", "arbitrary", "arbitrary")),
        name="flash_prompt",
    )(q, kb, vb, c_col)


def _page_decay_kernel(lf_ref, suffix_ref, total_ref):
    n_pages, heads, page = lf_ref.shape[1:]
    r = lax.broadcasted_iota(jnp.int32, (page, page), 0)
    c = lax.broadcasted_iota(jnp.int32, (page, page), 1)
    later = (r > c).astype(F32)
    x = lf_ref[0].reshape(n_pages * heads, page)
    suffix = jnp.dot(x, later, precision=lax.Precision.HIGHEST, preferred_element_type=F32)
    suffix_ref[0] = suffix.reshape(n_pages, heads, page)
    total_ref[0] = jnp.sum(x, axis=1, keepdims=True).reshape(n_pages, heads, 1)


def _page_decay(cache_lf_t):
    depth, pool, heads, page = cache_lf_t.shape
    pb = DECAY_PAGES
    return pl.pallas_call(
        _page_decay_kernel,
        grid=(depth, pool // pb),
        in_specs=[pl.BlockSpec((1, pb, heads, page), lambda d, i: (d, i, 0, 0))],
        out_specs=(pl.BlockSpec((1, pb, heads, page), lambda d, i: (d, i, 0, 0)),
                   pl.BlockSpec((1, pb, heads, 1), lambda d, i: (d, i, 0, 0))),
        out_shape=(jax.ShapeDtypeStruct((depth, pool, heads, page), F32),
                   jax.ShapeDtypeStruct((depth, pool, heads, 1), F32)),
        compiler_params=_cparams(("parallel", "parallel")),
        name="page_decay",
    )(cache_lf_t)


def _sample_attn_kernel(pt_ref, q_ref, kn_ref, vn_ref, lfc_ref, *rest):
    g = PAGES_PER_SEQ
    k_refs = rest[0:g]
    v_refs = rest[g:2 * g]
    d_refs = rest[2 * g:3 * g]
    t_refs = rest[3 * g:4 * g]
    o_ref = rest[4 * g]
    rows = q_ref.shape[1]
    page = k_refs[0].shape[2]
    keys = page * N_HEADS
    qb = q_ref[0].astype(BF16)
    row_head = lax.broadcasted_iota(jnp.int32, (rows, keys), 0) % N_HEADS
    key_head = lax.broadcasted_iota(jnp.int32, (rows, keys), 1) % N_HEADS
    head_match = row_head == key_head

    carry = jnp.zeros((rows, 1), F32)
    scores = []
    offsets = []
    for i in range(g):
        kb = k_refs[i][0, 0].reshape(keys, HEAD_DIM).astype(BF16)
        s = lax.dot_general(qb, kb, (((1,), (1,)), ((), ())), preferred_element_type=F32)
        scores.append(jnp.where(head_match, s + d_refs[i][0, 0] * LOG2E, NEG_BIG))
        offsets.append(carry * LOG2E)
        carry = carry + jnp.concatenate([t_refs[i][0, 0]] * (rows // N_HEADS), axis=0)
    values = [v_refs[i][0, 0].reshape(keys, HEAD_DIM).astype(BF16) for i in range(g)]

    kn = kn_ref[0].astype(BF16)
    s = lax.dot_general(qb, kn, (((1,), (1,)), ((), ())), preferred_element_type=F32)
    r = lax.broadcasted_iota(jnp.int32, (rows, rows), 0)
    c = lax.broadcasted_iota(jnp.int32, (rows, rows), 1)
    same_head = (r % N_HEADS) == (c % N_HEADS)
    upto = jnp.where(same_head & (r <= c), lfc_ref[0], 0.0)
    cn_row = jnp.sum(upto, axis=0, keepdims=True)
    scores.append(jnp.where(same_head & (c <= r), s - cn_row * LOG2E, NEG_BIG))
    offsets.append(jnp.zeros((rows, 1), F32))
    values.append(vn_ref[0].astype(BF16))

    m = jnp.max(scores[0], axis=1, keepdims=True) + offsets[0]
    for s, off in zip(scores[1:], offsets[1:]):
        m = jnp.maximum(m, jnp.max(s, axis=1, keepdims=True) + off)
    l = jnp.zeros((rows, 1), F32)
    acc = jnp.zeros((rows, HEAD_DIM), F32)
    for s, off, v_bf in zip(scores, offsets, values):
        p = jnp.exp2(s - (m - off))
        l = l + jnp.sum(p, axis=1, keepdims=True)
        acc = acc + jnp.dot(p.astype(BF16), v_bf, preferred_element_type=F32)
    o_ref[0] = acc / l


def _sample_attn(q, k_new, v_new, lf_new, cache_k, cache_v, decay, page_table, layer):
    suffix, totals = decay
    n, rows, _ = q.shape
    n_pages = page_table.shape[1]
    page = cache_k.shape[2]
    keys = page * N_HEADS
    g = PAGES_PER_SEQ
    assert n_pages == g

    def page_map5(p):
        return lambda b, pt: (layer, pt[b, n_pages - 1 - p], 0, 0, 0)

    def page_map4(p):
        return lambda b, pt: (layer, pt[b, n_pages - 1 - p], 0, 0)

    per_seq = lambda b, pt: (b, 0, 0)
    in_specs = [
        pl.BlockSpec((1, rows, HEAD_DIM), per_seq),
        pl.BlockSpec((1, rows, HEAD_DIM), per_seq),
        pl.BlockSpec((1, rows, HEAD_DIM), per_seq),
        pl.BlockSpec((1, rows, 1), per_seq),
    ]
    in_specs += [pl.BlockSpec((1, 1, page, N_HEADS, HEAD_DIM), page_map5(p)) for p in range(g)]
    in_specs += [pl.BlockSpec((1, 1, page, N_HEADS, HEAD_DIM), page_map5(p)) for p in range(g)]
    in_specs += [pl.BlockSpec((1, 1, 1, keys), page_map4(p)) for p in range(g)]
    in_specs += [pl.BlockSpec((1, 1, N_HEADS, 1), page_map4(p)) for p in range(g)]
    grid_spec = pltpu.PrefetchScalarGridSpec(
        num_scalar_prefetch=1,
        grid=(n,),
        in_specs=in_specs,
        out_specs=pl.BlockSpec((1, rows, HEAD_DIM), per_seq),
    )
    return pl.pallas_call(
        _sample_attn_kernel,
        grid_spec=grid_spec,
        out_shape=jax.ShapeDtypeStruct((n, rows, HEAD_DIM), F32),
        compiler_params=_cparams(("parallel",)),
        name="sample_attn",
    )(page_table, q, k_new, v_new, lf_new.reshape(n, rows, 1),
      *([cache_k] * g), *([cache_v] * g), *([suffix] * g), *([totals] * g))


def _discretise(a_re, a_im, log_dt):
    dt = jnp.exp(log_dt)
    mag = jnp.exp(dt * a_re)
    ab_re = mag * jnp.cos(dt * a_im)
    ab_im = mag * jnp.sin(dt * a_im)
    den = a_re * a_re + a_im * a_im
    f_re = ((ab_re - 1.0) * a_re + ab_im * a_im) / den
    f_im = (ab_im * a_re - (ab_re - 1.0) * a_im) / den
    return ab_re, ab_im, f_re, f_im


def _s5_prep_kernel(pe_ref, b_ref, pf_ref, c_ref, ps_ref,
                    e_ref, f_ref, m_ref, pw_ref, e32_ref, f0_ref):
    t_len = S5_CHUNK
    sc = STATE_COLS
    ab_re, ab_im, f_re, f_im = _discretise(pe_ref[0, 0], pe_ref[0, 1], pe_ref[0, 2])
    b_re = b_ref[0, 0]
    b_im = b_ref[0, 1]
    row_g = lax.broadcasted_iota(jnp.int32, (LANES, sc), 0) // SSM_GROUP
    col_g = lax.broadcasted_iota(jnp.int32, (LANES, sc), 1) // SSM_STATE
    same = row_g == col_g
    w_re = jnp.where(same, f_re * b_re - f_im * b_im, 0.0)
    w_im = jnp.where(same, f_re * b_im + f_im * b_re, 0.0)
    for j in range(t_len):
        s = t_len - 1 - j
        e32_ref[s * LANES:(s + 1) * LANES, 0:sc] = w_re
        e32_ref[s * LANES:(s + 1) * LANES, sc:2 * sc] = w_im
        w_re, w_im = ab_re * w_re - ab_im * w_im, ab_re * w_im + ab_im * w_re
    e_ref[0] = e32_ref[...].astype(BF16)

    ab_re, ab_im, _, _ = _discretise(pf_ref[0, 0], pf_ref[0, 1], pf_ref[0, 2])
    row_g = lax.broadcasted_iota(jnp.int32, (sc, LANES), 0) // SSM_STATE
    col_g = lax.broadcasted_iota(jnp.int32, (sc, LANES), 1) // SSM_GROUP
    same = row_g == col_g
    g_re = jnp.where(same, c_ref[0, 0], 0.0)
    g_im = jnp.where(same, c_ref[0, 1], 0.0)
    f0_ref[0:sc, :] = g_re
    f0_ref[sc:2 * sc, :] = -g_im
    for t in range(t_len):
        g_re, g_im = g_re * ab_re - g_im * ab_im, g_re * ab_im + g_im * ab_re
        f_ref[0, 0:sc, t * LANES:(t + 1) * LANES] = g_re.astype(BF16)
        f_ref[0, sc:2 * sc, t * LANES:(t + 1) * LANES] = (-g_im).astype(BF16)

    lag = jnp.dot(e32_ref[...], f0_ref[...], precision=lax.Precision.HIGHEST,
                  preferred_element_type=F32).astype(BF16)
    m_ref[0] = jnp.zeros(m_ref.shape[1:], BF16)
    for t in range(t_len):
        m_ref[0, 0:(t + 1) * LANES, t * LANES:(t + 1) * LANES] = lag[(t_len - 1 - t) * LANES:, :]

    p_re, p_im, _, _ = _discretise(ps_ref[0, 0:1], ps_ref[0, 1:2], ps_ref[0, 2:3])
    power = 1
    while power < S5_CHUNK:
        p_re, p_im = p_re * p_re - p_im * p_im, 2.0 * p_re * p_im
        power *= 2
        row = {DECODE_CHUNK: 0, S5_CHUNK: 1}.get(power)
        if row is not None:
            pw_ref[0, row:row + 1, 0:sc] = p_re
            pw_ref[0, row:row + 1, sc:2 * sc] = p_im


def _s5_prep(a_re, a_im, log_dt, b_re, b_im, c_re, c_im):
    nb, gb, st, ch = N_LANE_BLOCKS, GROUPS_PER_BLOCK, SSM_STATE, SSM_GROUP
    ldt = jnp.broadcast_to(log_dt[:, None], (SSM_GROUPS, st))
    par = jnp.stack([a_re, a_im, ldt], axis=0).reshape(3, nb, gb, st)
    pe = jnp.broadcast_to(par[:, :, :, None, None, :], (3, nb, gb, ch, gb, st))
    pe = pe.reshape(3, nb, LANES, STATE_COLS).transpose(1, 0, 2, 3)
    bb = jnp.stack([b_re, b_im], axis=0).reshape(2, nb, gb, st, ch).transpose(0, 1, 2, 4, 3)
    bb = jnp.broadcast_to(bb[:, :, :, :, None, :], (2, nb, gb, ch, gb, st))
    bb = bb.reshape(2, nb, LANES, STATE_COLS).transpose(1, 0, 2, 3)
    pf = jnp.broadcast_to(par[:, :, :, :, None, None], (3, nb, gb, st, gb, ch))
    pf = pf.reshape(3, nb, STATE_COLS, LANES).transpose(1, 0, 2, 3)
    cc = jnp.stack([c_re, c_im], axis=0).reshape(2, nb, gb, ch, st).transpose(0, 1, 2, 4, 3)
    cc = jnp.broadcast_to(cc[:, :, :, :, None, :], (2, nb, gb, st, gb, ch))
    cc = cc.reshape(2, nb, STATE_COLS, LANES).transpose(1, 0, 2, 3)
    ps = par.reshape(3, nb, STATE_COLS).transpose(1, 0, 2)

    tl = S5_CHUNK * LANES
    blk4 = lambda i: (i, 0, 0, 0)
    blk3 = lambda i: (i, 0, 0)
    return pl.pallas_call(
        _s5_prep_kernel,
        grid=(nb,),
        in_specs=[
            pl.BlockSpec((1, 3, LANES, STATE_COLS), blk4),
            pl.BlockSpec((1, 2, LANES, STATE_COLS), blk4),
            pl.BlockSpec((1, 3, STATE_COLS, LANES), blk4),
            pl.BlockSpec((1, 2, STATE_COLS, LANES), blk4),
            pl.BlockSpec((1, 3, STATE_COLS), blk3),
        ],
        out_specs=(
            pl.BlockSpec((1, tl, 2 * STATE_COLS), blk3),
            pl.BlockSpec((1, 2 * STATE_COLS, tl), blk3),
            pl.BlockSpec((1, tl, tl), blk3),
            pl.BlockSpec((1, 2, 2 * STATE_COLS), blk3),
        ),
        out_shape=(
            jax.ShapeDtypeStruct((nb, tl, 2 * STATE_COLS), BF16),
            jax.ShapeDtypeStruct((nb, 2 * STATE_COLS, tl), BF16),
            jax.ShapeDtypeStruct((nb, tl, tl), BF16),
            jax.ShapeDtypeStruct((nb, 2, 2 * STATE_COLS), F32),
        ),
        scratch_shapes=[pltpu.VMEM((tl, 2 * STATE_COLS), F32),
                        pltpu.VMEM((2 * STATE_COLS, LANES), F32)],
        compiler_params=_cparams(("parallel",)),
        name="s5_prep",
    )(pe, bb, pf, cc, ps)


def _s5_prompt_kernel(u_ref, e_ref, m_ref, f_ref, pw_ref, d_ref,
                      y_ref, hl_ref, ub_ref, sloc_ref, hp_ref, hpb_ref):
    ct = pl.program_id(1)
    n_rows = u_ref.shape[1]
    n_batch = hl_ref.shape[1]
    per_batch = n_rows // n_batch
    sc = STATE_COLS

    @pl.when(ct == 0)
    def _():
        ub = u_ref[0].astype(BF16)
        ub_ref[...] = ub
        sloc_ref[...] = jnp.dot(ub, e_ref[0], preferred_element_type=F32)
        p_re = pw_ref[0, 1:2, 0:sc]
        p_im = pw_ref[0, 1:2, sc:2 * sc]

        def body(k, carry):
            new = []
            for b in range(n_batch):
                h_re, h_im = carry[2 * b], carry[2 * b + 1]
                row = b * per_batch + k
                hp_ref[pl.ds(row, 1), 0:sc] = h_re
                hp_ref[pl.ds(row, 1), sc:2 * sc] = h_im
                s_re = sloc_ref[pl.ds(row, 1), 0:sc]
                s_im = sloc_ref[pl.ds(row, 1), sc:2 * sc]
                new.append(p_re * h_re - p_im * h_im + s_re)
                new.append(p_re * h_im + p_im * h_re + s_im)
            return tuple(new)

        zero = jnp.zeros((1, sc), F32)
        final = lax.fori_loop(0, per_batch, body, (zero,) * (2 * n_batch))
        for b in range(n_batch):
            hl_ref[0, b:b + 1, 0:sc] = final[2 * b]
            hl_ref[0, b:b + 1, sc:2 * sc] = final[2 * b + 1]
        hpb_ref[...] = hp_ref[...].astype(BF16)

    w = y_ref.shape[2]
    col = pl.multiple_of(ct * w, w)
    y = jnp.dot(ub_ref[...], m_ref[0], preferred_element_type=F32)
    y = y + jnp.dot(hpb_ref[...], f_ref[0], preferred_element_type=F32)
    y_ref[0] = y + d_ref[0] * u_ref[0, :, pl.ds(col, w)]


def _s5_prompt(u_chunks, e_op, m_op, f_op, pw, d_tiled, n_batch):
    nb, n_rows, tl = u_chunks.shape
    w = S5_COL_TILE
    return pl.pallas_call(
        _s5_prompt_kernel,
        grid=(nb, tl // w),
        in_specs=[
            pl.BlockSpec((1, n_rows, tl), lambda i, c: (i, 0, 0)),
            pl.BlockSpec((1, tl, 2 * STATE_COLS), lambda i, c: (i, 0, 0)),
            pl.BlockSpec((1, tl, w), lambda i, c: (i, 0, c)),
            pl.BlockSpec((1, 2 * STATE_COLS, w), lambda i, c: (i, 0, c)),
            pl.BlockSpec((1, 2, 2 * STATE_COLS), lambda i, c: (i, 0, 0)),
            pl.BlockSpec((1, 1, w), lambda i, c: (i, 0, 0)),
        ],
        out_specs=(
            pl.BlockSpec((1, n_rows, w), lambda i, c: (i, 0, c)),
            pl.BlockSpec((1, n_batch, 2 * STATE_COLS), lambda i, c: (i, 0, 0)),
        ),
        out_shape=(
            jax.ShapeDtypeStruct((nb, n_rows, tl), F32),
            jax.ShapeDtypeStruct((nb, n_batch, 2 * STATE_COLS), F32),
        ),
        scratch_shapes=[pltpu.VMEM((n_rows, tl), BF16),
                        pltpu.VMEM((n_rows, 2 * STATE_COLS), F32),
                        pltpu.VMEM((n_rows, 2 * STATE_COLS), F32),
                        pltpu.VMEM((n_rows, 2 * STATE_COLS), BF16)],
        compiler_params=_cparams(("parallel", "arbitrary")),
        name="s5_prompt",
    )(u_chunks, e_op, m_op, f_op, pw, d_tiled)


def _s5_sample_kernel(u_ref, e_ref, m_ref, f_ref, pw_ref, d_ref, h0_ref, y_ref, hl_ref):
    sc = STATE_COLS
    u = u_ref[0]
    ub = u.astype(BF16)
    h0 = h0_ref[0]
    y = jnp.dot(ub, m_ref[0], preferred_element_type=F32)
    y = y + jnp.dot(h0.astype(BF16), f_ref[0], preferred_element_type=F32)
    y_ref[0] = y + d_ref[0] * u
    sloc = jnp.dot(ub, e_ref[0], preferred_element_type=F32)
    p_re = pw_ref[0, 0:1, 0:sc]
    p_im = pw_ref[0, 0:1, sc:2 * sc]
    h_re = h0[:, 0:sc]
    h_im = h0[:, sc:2 * sc]
    hl_ref[0, :, 0:sc] = p_re * h_re - p_im * h_im + sloc[:, 0:sc]
    hl_ref[0, :, sc:2 * sc] = p_re * h_im + p_im * h_re + sloc[:, sc:2 * sc]


def _s5_sample(u_chunks, e_op, m_op, f_op, pw, d_tiled, h0):
    nb, n_seq, w = u_chunks.shape
    tl = e_op.shape[1]
    e_blk = (tl - w) // w
    return pl.pallas_call(
        _s5_sample_kernel,
        grid=(nb,),
        in_specs=[
            pl.BlockSpec((1, n_seq, w), lambda i: (i, 0, 0)),
            pl.BlockSpec((1, w, 2 * STATE_COLS), lambda i: (i, e_blk, 0)),
            pl.BlockSpec((1, w, w), lambda i: (i, 0, 0)),
            pl.BlockSpec((1, 2 * STATE_COLS, w), lambda i: (i, 0, 0)),
            pl.BlockSpec((1, 2, 2 * STATE_COLS), lambda i: (i, 0, 0)),
            pl.BlockSpec((1, 1, w), lambda i: (i, 0, 0)),
            pl.BlockSpec((1, n_seq, 2 * STATE_COLS), lambda i: (i, 0, 0)),
        ],
        out_specs=(
            pl.BlockSpec((1, n_seq, w), lambda i: (i, 0, 0)),
            pl.BlockSpec((1, n_seq, 2 * STATE_COLS), lambda i: (i, 0, 0)),
        ),
        out_shape=(
            jax.ShapeDtypeStruct((nb, n_seq, w), F32),
            jax.ShapeDtypeStruct((nb, n_seq, 2 * STATE_COLS), F32),
        ),
        compiler_params=_cparams(("parallel",)),
        name="s5_sample",
    )(u_chunks, e_op, m_op, f_op, pw, d_tiled, h0)


def _mix_out_kernel(x_ref, attn_ref, y_ref, gw_ref, gb_ref, ga_ref, gs_ref, wo_ref, gp_ref,
                    o_ref, ycat_ref, mg_ref):
    for lb in range(N_LANE_BLOCKS):
        ycat_ref[:, lb * LANES:(lb + 1) * LANES] = jax.nn.gelu(y_ref[lb])
    y = ycat_ref[...]
    z = jnp.dot(y.astype(BF16), gw_ref[...], preferred_element_type=F32) + gb_ref[...]
    s = y * jax.nn.sigmoid(z)
    mg_ref[:, 0:D_ATTN] = _rms(attn_ref[...], ga_ref[...]).astype(BF16)
    mg_ref[:, D_ATTN:] = _rms(s, gs_ref[...]).astype(BF16)
    m = jnp.dot(mg_ref[...], wo_ref[...], preferred_element_type=F32)
    o_ref[...] = x_ref[...] + _rms(m, gp_ref[...])


def _mix_out(x, attn, y_blk, glu_w, glu_b, g_attn, g_ssm, w_out, g_post, layer):
    m = x.shape[0]
    tm = MIX_OUT_ROWS
    row = lambda i: (i, 0)
    const = lambda i: (0, 0)
    per_layer = lambda i: (layer, 0, 0)
    return pl.pallas_call(
        _mix_out_kernel,
        grid=(m // tm,),
        in_specs=[
            pl.BlockSpec((tm, D_MODEL), row),
            pl.BlockSpec((tm, D_ATTN), row),
            pl.BlockSpec((N_LANE_BLOCKS, tm, LANES), lambda i: (0, i, 0)),
            pl.BlockSpec((None, D_SSM, D_SSM), per_layer),
            pl.BlockSpec((1, D_SSM), const),
            pl.BlockSpec((1, D_ATTN), const),
            pl.BlockSpec((1, D_SSM), const),
            pl.BlockSpec((None, D_MODEL, D_MODEL), per_layer),
            pl.BlockSpec((1, D_MODEL), const),
        ],
        out_specs=pl.BlockSpec((tm, D_MODEL), row),
        out_shape=jax.ShapeDtypeStruct((m, D_MODEL), F32),
        scratch_shapes=[pltpu.VMEM((tm, D_SSM), F32), pltpu.VMEM((tm, D_MODEL), BF16)],
        compiler_params=_cparams(("parallel",)),
        name="mix_out",
    )(x, attn, y_blk, glu_w, glu_b, g_attn, g_ssm, w_out, g_post)


def _state_to_blocks(h):
    n = h.shape[0]
    h = h.reshape(n, N_LANE_BLOCKS, STATE_COLS, 2)
    return h.transpose(1, 0, 3, 2).reshape(N_LANE_BLOCKS, n, 2 * STATE_COLS)


def _blocks_to_state(h):
    n = h.shape[1]
    h = h.reshape(N_LANE_BLOCKS, n, 2, STATE_COLS).transpose(1, 0, 3, 2)
    return h.reshape(n, SSM_GROUPS, SSM_STATE, 2)


def _project_in(x, l, w):
    u_blk, q, k, v, kb, vb, lf = _mix_in(x, w["ng"][l][2], w["w_in"], w["w_f"], w["b_f"], l)
    return u_blk, q, k, v, kb, vb, lf[:, :N_HEADS]


def _project_out(x, attn, y_blk, l, w):
    return _mix_out(x, attn, y_blk, w["glu_w"], w["glu_b"][l], w["g_attn"][l], w["g_ssm"][l],
                    w["w_out"], w["ng"][l][3], l)


def kernel(x_prompt, x_sample, cache_k, cache_v, cache_logf, state_ssm, page_table, norm_g, ffn_gate, ffn_up, ffn_down, w_in, b_forget, ssm_a_re, ssm_a_im, ssm_log_dt, ssm_b_re, ssm_b_im, ssm_c_re, ssm_c_im, ssm_d, glu_w, glu_b, g_attn_out, g_ssm_out, w_out):
    batch, seq, _ = x_prompt.shape
    n_seq, n_new, _ = x_sample.shape
    depth = norm_g.shape[0]
    n_pool, page = cache_k.shape[1], cache_k.shape[2]
    assert seq % ROW_TILE == 0 and seq % ATT_TILE == 0 and seq % S5_CHUNK == 0
    assert (n_seq * n_new) % 256 == 0 and page_table.shape[1] == PAGES_PER_SEQ
    assert n_pool % DECAY_PAGES == 0 and N_HEADS == SUBLANES and n_new == DECODE_CHUNK

    xp = x_prompt.reshape(batch * seq, D_MODEL)
    xs = x_sample.reshape(n_seq * n_new, D_MODEL)

    n_main = D_SSM + 3 * D_ATTN
    w = {
        "ng": [[norm_g[l, i].reshape(1, D_MODEL) for i in range(6)] for l in range(depth)],
        "wg": ffn_gate.astype(BF16),
        "wu": ffn_up.astype(BF16),
        "wd": ffn_down.astype(BF16),
        "w_in": w_in.astype(BF16),
        "w_f": jnp.pad(w_in[:, :, n_main:], ((0, 0), (0, 0), (0, LANES - N_HEADS))).astype(BF16),
        "b_f": jnp.pad(b_forget, ((0, 0), (0, LANES - N_HEADS))).reshape(depth, 1, LANES),
        "glu_w": glu_w.astype(BF16),
        "glu_b": [glu_b[l].reshape(1, D_SSM) for l in range(depth)],
        "g_attn": [g_attn_out[l].reshape(1, D_ATTN) for l in range(depth)],
        "g_ssm": [g_ssm_out[l].reshape(1, D_SSM) for l in range(depth)],
        "w_out": w_out.astype(BF16),
    }

    suffix, totals = _page_decay(cache_logf.transpose(0, 1, 3, 2))
    decay = (suffix.transpose(0, 1, 3, 2).reshape(depth, n_pool, 1, page * N_HEADS), totals)

    outs = {name: [] for name in ("kp", "vp", "lfp", "sp", "ks", "vs", "lfs", "ss")}
    for l in range(depth):
        e_op, f_op, m_op, pw = _s5_prep(ssm_a_re[l], ssm_a_im[l], ssm_log_dt[l],
                                        ssm_b_re[l], ssm_b_im[l], ssm_c_re[l], ssm_c_im[l])
        d_blk = ssm_d[l].reshape(N_LANE_BLOCKS, 1, LANES)
        d_prompt = jnp.tile(d_blk, (1, 1, S5_COL_TILE // LANES))
        d_sample = jnp.tile(d_blk, (1, 1, n_new))

        ng = w["ng"][l]
        ffn_w = (w["wg"], w["wu"], w["wd"])
        rows = n_new * N_HEADS

        xp = _ffn_half(xp, ng[0], ng[1], *ffn_w, l, 0)
        u1, q1, k1, v1, kb1, vb1, lf1 = _project_in(xp, l, w)
        lf_t = lf1.reshape(batch, seq, N_HEADS).transpose(0, 2, 1)
        c_col = _cumsum_heads(lf_t).transpose(0, 2, 1)
        attn1 = _flash_prompt(q1.reshape(batch, seq, D_ATTN), kb1.reshape(batch, seq, D_ATTN),
                              vb1.reshape(batch, seq, D_ATTN), c_col)
        u_chunks = u1.reshape(N_LANE_BLOCKS, batch * seq // S5_CHUNK, S5_CHUNK * LANES)
        y1, h1 = _s5_prompt(u_chunks, e_op, m_op, f_op, pw, d_prompt, batch)
        s1 = _blocks_to_state(h1)
        xp = _project_out(xp, attn1.reshape(batch * seq, D_ATTN),
                          y1.reshape(N_LANE_BLOCKS, batch * seq, LANES), l, w)
        xp = _ffn_half(xp, ng[4], ng[5], *ffn_w, l, 1)

        xs = _ffn_half(xs, ng[0], ng[1], *ffn_w, l, 0)
        u2, q2, k2, v2, _, _, lf2 = _project_in(xs, l, w)
        attn2 = _sample_attn(q2.astype(F32).reshape(n_seq, rows, HEAD_DIM),
                             k2.reshape(n_seq, rows, HEAD_DIM), v2.reshape(n_seq, rows, HEAD_DIM),
                             lf2.reshape(n_seq, rows), cache_k, cache_v, decay, page_table, l)
        attn2 = attn2.reshape(n_seq * n_new, D_ATTN)
        y2, h2 = _s5_sample(u2.reshape(N_LANE_BLOCKS, n_seq, n_new * LANES), e_op, m_op, f_op, pw,
                            d_sample, _state_to_blocks(state_ssm[l]))
        s2 = _blocks_to_state(h2)
        xs = _project_out(xs, attn2, y2.reshape(N_LANE_BLOCKS, n_seq * n_new, LANES), l, w)
        xs = _ffn_half(xs, ng[4], ng[5], *ffn_w, l, 1)
        outs["kp"].append(k1.reshape(batch, seq, N_HEADS, HEAD_DIM))
        outs["vp"].append(v1.reshape(batch, seq, N_HEADS, HEAD_DIM))
        outs["lfp"].append(lf1.reshape(batch, seq, N_HEADS))
        outs["sp"].append(s1)
        outs["ks"].append(k2.reshape(n_seq, n_new, N_HEADS, HEAD_DIM))
        outs["vs"].append(v2.reshape(n_seq, n_new, N_HEADS, HEAD_DIM))
        outs["lfs"].append(lf2.reshape(n_seq, n_new, N_HEADS))
        outs["ss"].append(s2)

    return (xp.reshape(batch, seq, D_MODEL), xs.reshape(n_seq, n_new, D_MODEL),
            jnp.stack(outs["kp"]), jnp.stack(outs["vp"]), jnp.stack(outs["lfp"]),
            jnp.stack(outs["sp"]),
            jnp.stack(outs["ks"]), jnp.stack(outs["vs"]), jnp.stack(outs["lfs"]),
            jnp.stack(outs["ss"]))
```

```python
import math

import jax
import jax.numpy as jnp
from jax import lax
from jax.experimental import pallas as pl
from jax.experimental.pallas import tpu as pltpu

F32 = jnp.float32
BF16 = jnp.bfloat16

D_MODEL = 2048
D_SSM = 1024
D_ATTN = 1024
HEAD_DIM = 128
N_HEADS = 8
SSM_GROUP = 16
SSM_GROUPS = 64
SSM_STATE = 64
D_FF = 5504
EPS = 1e-6
LOG2E = math.log2(math.e)

LANES = 128
SUBLANES = 8
GROUPS_PER_BLOCK = LANES // SSM_GROUP
N_LANE_BLOCKS = D_SSM // LANES
STATE_COLS = GROUPS_PER_BLOCK * SSM_STATE
S5_CHUNK = 16
DECODE_CHUNK = 4
S5_COL_TILE = 512
MIX_OUT_ROWS = 256
FF_TILE = 1024
FF_STEPS = (D_FF + FF_TILE - 1) // FF_TILE
FF_TAIL = D_FF - (FF_STEPS - 1) * FF_TILE
ROW_TILE = 512
ATT_TILE = 512
ATT_COLS = 512
HEADS_PER_STEP = 4
PAGES_PER_SEQ = 16
DECAY_PAGES = 256
VMEM_LIMIT = 56 * 1024 * 1024
NEG_BIG = -1e30


def _cparams(sem):
    return pltpu.CompilerParams(dimension_semantics=sem, vmem_limit_bytes=VMEM_LIMIT)


def _rms(x, g):
    ms = jnp.mean(x * x, axis=-1, keepdims=True)
    return x * lax.rsqrt(ms + EPS) * g


def _log_sigmoid(z):
    return jnp.minimum(z, 0.0) - jnp.log1p(jnp.exp(-jnp.abs(z)))


def _ffn_kernel(x_ref, gpre_ref, gpost_ref, wg_ref, wu_ref, wd_ref, o_ref, h_ref):
    j = pl.program_id(1)
    last = pl.num_programs(1) - 1
    acc_ref = o_ref

    @pl.when(j == 0)
    def _():
        h_ref[...] = _rms(x_ref[...], gpre_ref[...]).astype(BF16)
        acc_ref[...] = jnp.zeros_like(acc_ref)

    def hidden_tile(width):
        h = h_ref[...]
        g = jnp.dot(h, wg_ref[:, 0:width], preferred_element_type=F32)
        u = jnp.dot(h, wu_ref[:, 0:width], preferred_element_type=F32)
        a = (g * jax.nn.sigmoid(g) * u).astype(BF16)
        acc_ref[...] += jnp.dot(a, wd_ref[0:width, :], preferred_element_type=F32)

    if FF_TAIL == FF_TILE:
        hidden_tile(FF_TILE)
    else:
        pl.when(j < last)(lambda: hidden_tile(FF_TILE))
        pl.when(j == last)(lambda: hidden_tile(FF_TAIL))

    @pl.when(j == last)
    def _():
        o_ref[...] = x_ref[...] + 0.5 * _rms(acc_ref[...], gpost_ref[...])


def _ffn_half(x, g_pre, g_post, wg, wu, wd, layer, half):
    m = x.shape[0]
    tm = min(ROW_TILE, m)
    return pl.pallas_call(
        _ffn_kernel,
        grid=(m // tm, FF_STEPS),
        in_specs=[
            pl.BlockSpec((tm, D_MODEL), lambda i, j: (i, 0)),
            pl.BlockSpec((1, D_MODEL), lambda i, j: (0, 0)),
            pl.BlockSpec((1, D_MODEL), lambda i, j: (0, 0)),
            pl.BlockSpec((None, None, D_MODEL, FF_TILE), lambda i, j: (layer, half, 0, j)),
            pl.BlockSpec((None, None, D_MODEL, FF_TILE), lambda i, j: (layer, half, 0, j)),
            pl.BlockSpec((None, None, FF_TILE, D_MODEL), lambda i, j: (layer, half, j, 0)),
        ],
        out_specs=pl.BlockSpec((tm, D_MODEL), lambda i, j: (i, 0)),
        out_shape=jax.ShapeDtypeStruct((m, D_MODEL), F32),
        scratch_shapes=[pltpu.VMEM((tm, D_MODEL), BF16)],
        compiler_params=_cparams(("parallel", "arbitrary")),
        name="ffn_half",
    )(x, g_pre, g_post, wg, wu, wd)


def _mix_in_kernel(x_ref, g_ref, w_ref, wf_ref, bf_ref,
                   u_ref, q_ref, k_ref, v_ref, kb_ref, vb_ref, lf_ref, h_ref):
    j = pl.program_id(1)

    @pl.when(j == 0)
    def _():
        h = _rms(x_ref[...], g_ref[...]).astype(BF16)
        h_ref[...] = h
        z = jnp.dot(h, wf_ref[...], preferred_element_type=F32) + bf_ref[...]
        lf_ref[...] = _log_sigmoid(z)

    p = jnp.dot(h_ref[...], w_ref[...], preferred_element_type=F32)

    @pl.when(j == 0)
    def _():
        for lb in range(N_LANE_BLOCKS):
            u_ref[lb] = p[:, lb * LANES:(lb + 1) * LANES]

    @pl.when(j == 1)
    def _():
        q_ref[...] = (p * (HEAD_DIM ** -0.5 * LOG2E)).astype(BF16)

    @pl.when(j == 2)
    def _():
        k_ref[...] = p
        kb_ref[...] = p.astype(BF16)

    @pl.when(j == 3)
    def _():
        v_ref[...] = p
        vb_ref[...] = p.astype(BF16)


def _mix_in(x, g, w_main, w_f, b_f, layer):
    m = x.shape[0]
    tm = min(ROW_TILE, m)
    row = lambda i, j: (i, 0)
    out_shape = (
        jax.ShapeDtypeStruct((N_LANE_BLOCKS, m, LANES), F32),
        jax.ShapeDtypeStruct((m, D_ATTN), BF16),
        jax.ShapeDtypeStruct((m, D_ATTN), F32),
        jax.ShapeDtypeStruct((m, D_ATTN), F32),
        jax.ShapeDtypeStruct((m, D_ATTN), BF16),
        jax.ShapeDtypeStruct((m, D_ATTN), BF16),
        jax.ShapeDtypeStruct((m, LANES), F32),
    )
    out_specs = (
        pl.BlockSpec((N_LANE_BLOCKS, tm, LANES), lambda i, j: (0, i, 0)),
        pl.BlockSpec((tm, D_ATTN), row),
        pl.BlockSpec((tm, D_ATTN), row),
        pl.BlockSpec((tm, D_ATTN), row),
        pl.BlockSpec((tm, D_ATTN), row),
        pl.BlockSpec((tm, D_ATTN), row),
        pl.BlockSpec((tm, LANES), row),
    )
    return pl.pallas_call(
        _mix_in_kernel,
        grid=(m // tm, 4),
        in_specs=[
            pl.BlockSpec((tm, D_MODEL), row),
            pl.BlockSpec((1, D_MODEL), lambda i, j: (0, 0)),
            pl.BlockSpec((None, D_MODEL, D_ATTN), lambda i, j: (layer, 0, j)),
            pl.BlockSpec((None, D_MODEL, LANES), lambda i, j: (layer, 0, 0)),
            pl.BlockSpec((None, 1, LANES), lambda i, j: (layer, 0, 0)),
        ],
        out_specs=out_specs,
        out_shape=out_shape,
        scratch_shapes=[pltpu.VMEM((tm, D_MODEL), BF16)],
        compiler_params=_cparams(("parallel", "arbitrary")),
        name="mix_in",
    )(x, g, w_main, w_f, b_f)


def _cumsum_kernel(x_ref, o_ref):
    blk = 512
    n = x_ref.shape[2]
    r = lax.broadcasted_iota(jnp.int32, (blk, blk), 0)
    c = lax.broadcasted_iota(jnp.int32, (blk, blk), 1)
    tri = (r <= c).astype(F32)
    carry = jnp.zeros((N_HEADS, 1), F32)
    for b in range(n // blk):
        xb = x_ref[0, :, b * blk:(b + 1) * blk]
        cb = jnp.dot(xb, tri, precision=lax.Precision.HIGHEST,
                     preferred_element_type=F32) + carry
        o_ref[0, :, b * blk:(b + 1) * blk] = cb * LOG2E
        carry = cb[:, blk - 1:blk]


def _cumsum_heads(lf_t):
    b, h, n = lf_t.shape
    return pl.pallas_call(
        _cumsum_kernel,
        grid=(b,),
        in_specs=[pl.BlockSpec((1, h, n), lambda i: (i, 0, 0))],
        out_specs=pl.BlockSpec((1, h, n), lambda i: (i, 0, 0)),
        out_shape=jax.ShapeDtypeStruct((b, h, n), F32),
        compiler_params=_cparams(("parallel",)),
        name="logf_cumsum",
    )(lf_t)


def _flash_kernel(q_ref, k_ref, v_ref, ccol_ref, o_ref, cb_ref, m_ref, l_ref, acc_ref):
    hb = pl.program_id(1)
    qi = pl.program_id(2)
    t = ATT_TILE
    n = k_ref.shape[1]

    @pl.when(qi == 0)
    def _():
        cc = ccol_ref[0]
        lane = lax.broadcasted_iota(jnp.int32, cc.shape, 1)
        for a in range(HEADS_PER_STEP):
            col = jnp.sum(jnp.where(lane == hb * HEADS_PER_STEP + a, cc, 0.0),
                          axis=1, keepdims=True)
            cb_ref[a] = jnp.broadcast_to(col, (n, LANES))

    m_ref[...] = jnp.full_like(m_ref, NEG_BIG)
    l_ref[...] = jnp.zeros_like(l_ref)
    acc_ref[...] = jnp.zeros_like(acc_ref)

    def step(ki, masked):
        start = pl.multiple_of(ki * t, t)
        for a in range(HEADS_PER_STEP):
            hl = slice(a * HEAD_DIM, (a + 1) * HEAD_DIM)
            k = k_ref[0, pl.ds(start, t), hl]
            v = v_ref[0, pl.ds(start, t), hl]
            cb = cb_ref[a, pl.ds(start, t), :]
            cb = jnp.concatenate([cb] * (ATT_COLS // LANES), axis=1)
            for j in range(t // ATT_COLS):
                cols = slice(j * ATT_COLS, (j + 1) * ATT_COLS)
                q = q_ref[0, cols, hl]
                s = lax.dot_general(k, q, (((1,), (1,)), ((), ())),
                                    preferred_element_type=F32) - cb
                if masked:
                    r = lax.broadcasted_iota(jnp.int32, s.shape, 0)
                    c = lax.broadcasted_iota(jnp.int32, s.shape, 1) + j * ATT_COLS
                    s = jnp.where(r <= c, s, NEG_BIG)
                m_old = m_ref[a, :, cols]
                m_new = jnp.maximum(m_old, jnp.max(s, axis=0, keepdims=True))
                alpha = jnp.exp2(m_old - m_new)
                p = jnp.exp2(s - m_new)
                l_ref[a, :, cols] = alpha * l_ref[a, :, cols] + jnp.sum(p, axis=0, keepdims=True)
                pv = lax.dot_general(v, p.astype(BF16), (((0,), (0,)), ((), ())),
                                     preferred_element_type=F32)
                acc_ref[a, :, cols] = alpha * acc_ref[a, :, cols] + pv
                m_ref[a, :, cols] = m_new

    def body(ki, carry):
        step(ki, False)
        return carry

    lax.fori_loop(0, qi, body, 0)
    step(qi, True)
    for a in range(HEADS_PER_STEP):
        o_ref[0, :, a * HEAD_DIM:(a + 1) * HEAD_DIM] = (acc_ref[a] / l_ref[a]).T


def _flash_prompt(q, kb, vb, c_col):
    b, n, _ = q.shape
    t = ATT_TILE
    w = HEADS_PER_STEP * HEAD_DIM
    return pl.pallas_call(
        _flash_kernel,
        grid=(b, N_HEADS // HEADS_PER_STEP, n // t),
        in_specs=[
            pl.BlockSpec((1, t, w), lambda bi, h, qi: (bi, qi, h)),
            pl.BlockSpec((1, n, w), lambda bi, h, qi: (bi, 0, h)),
            pl.BlockSpec((1, n, w), lambda bi, h, qi: (bi, 0, h)),
            pl.BlockSpec((1, n, N_HEADS), lambda bi, h, qi: (bi, 0, 0)),
        ],
        out_specs=pl.BlockSpec((1, t, w), lambda bi, h, qi: (bi, qi, h)),
        out_shape=jax.ShapeDtypeStruct((b, n, D_ATTN), F32),
        scratch_shapes=[pltpu.VMEM((HEADS_PER_STEP, n, LANES), F32),
                        pltpu.VMEM((HEADS_PER_STEP, 1, t), F32),
                        pltpu.VMEM((HEADS_PER_STEP, 1, t), F32),
                        pltpu.VMEM((HEADS_PER_STEP, HEAD_DIM, t), F32)],
        compiler_params=_cparams(("arbitrary", "arbitrary", "arbitrary")),
        name="flash_prompt",
    )(q, kb, vb, c_col)


def _page_decay_kernel(lf_ref, suffix_ref, total_ref):
    n_pages, heads, page = lf_ref.shape[1:]
    r = lax.broadcasted_iota(jnp.int32, (page, page), 0)
    c = lax.broadcasted_iota(jnp.int32, (page, page), 1)
    later = (r > c).astype(F32)
    x = lf_ref[0].reshape(n_pages * heads, page)
    suffix = jnp.dot(x, later, precision=lax.Precision.HIGHEST, preferred_element_type=F32)
    suffix_ref[0] = suffix.reshape(n_pages, heads, page)
    total_ref[0] = jnp.sum(x, axis=1, keepdims=True).reshape(n_pages, heads, 1)


def _page_decay(cache_lf_t):
    depth, pool, heads, page = cache_lf_t.shape
    pb = DECAY_PAGES
    return pl.pallas_call(
        _page_decay_kernel,
        grid=(depth, pool // pb),
        in_specs=[pl.BlockSpec((1, pb, heads, page), lambda d, i: (d, i, 0, 0))],
        out_specs=(pl.BlockSpec((1, pb, heads, page), lambda d, i: (d, i, 0, 0)),
                   pl.BlockSpec((1, pb, heads, 1), lambda d, i: (d, i, 0, 0))),
        out_shape=(jax.ShapeDtypeStruct((depth, pool, heads, page), F32),
                   jax.ShapeDtypeStruct((depth, pool, heads, 1), F32)),
        compiler_params=_cparams(("parallel", "parallel")),
        name="page_decay",
    )(cache_lf_t)


def _sample_attn_kernel(pt_ref, q_ref, kn_ref, vn_ref, lfc_ref, *rest):
    g = PAGES_PER_SEQ
    k_refs = rest[0:g]
    v_refs = rest[g:2 * g]
    d_refs = rest[2 * g:3 * g]
    t_refs = rest[3 * g:4 * g]
    o_ref = rest[4 * g]
    rows = q_ref.shape[1]
    page = k_refs[0].shape[2]
    keys = page * N_HEADS
    qb = q_ref[0].astype(BF16)
    row_head = lax.broadcasted_iota(jnp.int32, (rows, keys), 0) % N_HEADS
    key_head = lax.broadcasted_iota(jnp.int32, (rows, keys), 1) % N_HEADS
    head_match = row_head == key_head

    carry = jnp.zeros((rows, 1), F32)
    scores = []
    offsets = []
    for i in range(g):
        kb = k_refs[i][0, 0].reshape(keys, HEAD_DIM).astype(BF16)
        s = lax.dot_general(qb, kb, (((1,), (1,)), ((), ())), preferred_element_type=F32)
        scores.append(jnp.where(head_match, s + d_refs[i][0, 0] * LOG2E, NEG_BIG))
        offsets.append(carry * LOG2E)
        carry = carry + jnp.concatenate([t_refs[i][0, 0]] * (rows // N_HEADS), axis=0)
    values = [v_refs[i][0, 0].reshape(keys, HEAD_DIM).astype(BF16) for i in range(g)]

    kn = kn_ref[0].astype(BF16)
    s = lax.dot_general(qb, kn, (((1,), (1,)), ((), ())), preferred_element_type=F32)
    r = lax.broadcasted_iota(jnp.int32, (rows, rows), 0)
    c = lax.broadcasted_iota(jnp.int32, (rows, rows), 1)
    same_head = (r % N_HEADS) == (c % N_HEADS)
    upto = jnp.where(same_head & (r <= c), lfc_ref[0], 0.0)
    cn_row = jnp.sum(upto, axis=0, keepdims=True)
    scores.append(jnp.where(same_head & (c <= r), s - cn_row * LOG2E, NEG_BIG))
    offsets.append(jnp.zeros((rows, 1), F32))
    values.append(vn_ref[0].astype(BF16))

    m = jnp.max(scores[0], axis=1, keepdims=True) + offsets[0]
    for s, off in zip(scores[1:], offsets[1:]):
        m = jnp.maximum(m, jnp.max(s, axis=1, keepdims=True) + off)
    l = jnp.zeros((rows, 1), F32)
    acc = jnp.zeros((rows, HEAD_DIM), F32)
    for s, off, v_bf in zip(scores, offsets, values):
        p = jnp.exp2(s - (m - off))
        l = l + jnp.sum(p, axis=1, keepdims=True)
        acc = acc + jnp.dot(p.astype(BF16), v_bf, preferred_element_type=F32)
    o_ref[0] = acc / l


def _sample_attn(q, k_new, v_new, lf_new, cache_k, cache_v, decay, page_table, layer):
    suffix, totals = decay
    n, rows, _ = q.shape
    n_pages = page_table.shape[1]
    page = cache_k.shape[2]
    keys = page * N_HEADS
    g = PAGES_PER_SEQ
    assert n_pages == g

    def page_map5(p):
        return lambda b, pt: (layer, pt[b, n_pages - 1 - p], 0, 0, 0)

    def page_map4(p):
        return lambda b, pt: (layer, pt[b, n_pages - 1 - p], 0, 0)

    per_seq = lambda b, pt: (b, 0, 0)
    in_specs = [
        pl.BlockSpec((1, rows, HEAD_DIM), per_seq),
        pl.BlockSpec((1, rows, HEAD_DIM), per_seq),
        pl.BlockSpec((1, rows, HEAD_DIM), per_seq),
        pl.BlockSpec((1, rows, 1), per_seq),
    ]
    in_specs += [pl.BlockSpec((1, 1, page, N_HEADS, HEAD_DIM), page_map5(p)) for p in range(g)]
    in_specs += [pl.BlockSpec((1, 1, page, N_HEADS, HEAD_DIM), page_map5(p)) for p in range(g)]
    in_specs += [pl.BlockSpec((1, 1, 1, keys), page_map4(p)) for p in range(g)]
    in_specs += [pl.BlockSpec((1, 1, N_HEADS, 1), page_map4(p)) for p in range(g)]
    grid_spec = pltpu.PrefetchScalarGridSpec(
        num_scalar_prefetch=1,
        grid=(n,),
        in_specs=in_specs,
        out_specs=pl.BlockSpec((1, rows, HEAD_DIM), per_seq),
    )
    return pl.pallas_call(
        _sample_attn_kernel,
        grid_spec=grid_spec,
        out_shape=jax.ShapeDtypeStruct((n, rows, HEAD_DIM), F32),
        compiler_params=_cparams(("parallel",)),
        name="sample_attn",
    )(page_table, q, k_new, v_new, lf_new.reshape(n, rows, 1),
      *([cache_k] * g), *([cache_v] * g), *([suffix] * g), *([totals] * g))


def _discretise(a_re, a_im, log_dt):
    dt = jnp.exp(log_dt)
    mag = jnp.exp(dt * a_re)
    ab_re = mag * jnp.cos(dt * a_im)
    ab_im = mag * jnp.sin(dt * a_im)
    den = a_re * a_re + a_im * a_im
    f_re = ((ab_re - 1.0) * a_re + ab_im * a_im) / den
    f_im = (ab_im * a_re - (ab_re - 1.0) * a_im) / den
    return ab_re, ab_im, f_re, f_im


def _s5_prep_kernel(pe_ref, b_ref, pf_ref, c_ref, ps_ref,
                    e_ref, f_ref, m_ref, pw_ref, e32_ref, f0_ref):
    t_len = S5_CHUNK
    sc = STATE_COLS
    ab_re, ab_im, f_re, f_im = _discretise(pe_ref[0, 0], pe_ref[0, 1], pe_ref[0, 2])
    b_re = b_ref[0, 0]
    b_im = b_ref[0, 1]
    row_g = lax.broadcasted_iota(jnp.int32, (LANES, sc), 0) // SSM_GROUP
    col_g = lax.broadcasted_iota(jnp.int32, (LANES, sc), 1) // SSM_STATE
    same = row_g == col_g
    w_re = jnp.where(same, f_re * b_re - f_im * b_im, 0.0)
    w_im = jnp.where(same, f_re * b_im + f_im * b_re, 0.0)
    for j in range(t_len):
        s = t_len - 1 - j
        e32_ref[s * LANES:(s + 1) * LANES, 0:sc] = w_re
        e32_ref[s * LANES:(s + 1) * LANES, sc:2 * sc] = w_im
        w_re, w_im = ab_re * w_re - ab_im * w_im, ab_re * w_im + ab_im * w_re
    e_ref[0] = e32_ref[...].astype(BF16)

    ab_re, ab_im, _, _ = _discretise(pf_ref[0, 0], pf_ref[0, 1], pf_ref[0, 2])
    row_g = lax.broadcasted_iota(jnp.int32, (sc, LANES), 0) // SSM_STATE
    col_g = lax.broadcasted_iota(jnp.int32, (sc, LANES), 1) // SSM_GROUP
    same = row_g == col_g
    g_re = jnp.where(same, c_ref[0, 0], 0.0)
    g_im = jnp.where(same, c_ref[0, 1], 0.0)
    f0_ref[0:sc, :] = g_re
    f0_ref[sc:2 * sc, :] = -g_im
    for t in range(t_len):
        g_re, g_im = g_re * ab_re - g_im * ab_im, g_re * ab_im + g_im * ab_re
        f_ref[0, 0:sc, t * LANES:(t + 1) * LANES] = g_re.astype(BF16)
        f_ref[0, sc:2 * sc, t * LANES:(t + 1) * LANES] = (-g_im).astype(BF16)

    lag = jnp.dot(e32_ref[...], f0_ref[...], precision=lax.Precision.HIGHEST,
                  preferred_element_type=F32).astype(BF16)
    m_ref[0] = jnp.zeros(m_ref.shape[1:], BF16)
    for t in range(t_len):
        m_ref[0, 0:(t + 1) * LANES, t * LANES:(t + 1) * LANES] = lag[(t_len - 1 - t) * LANES:, :]

    p_re, p_im, _, _ = _discretise(ps_ref[0, 0:1], ps_ref[0, 1:2], ps_ref[0, 2:3])
    power = 1
    while power < S5_CHUNK:
        p_re, p_im = p_re * p_re - p_im * p_im, 2.0 * p_re * p_im
        power *= 2
        row = {DECODE_CHUNK: 0, S5_CHUNK: 1}.get(power)
        if row is not None:
            pw_ref[0, row:row + 1, 0:sc] = p_re
            pw_ref[0, row:row + 1, sc:2 * sc] = p_im


def _s5_prep(a_re, a_im, log_dt, b_re, b_im, c_re, c_im):
    nb, gb, st, ch = N_LANE_BLOCKS, GROUPS_PER_BLOCK, SSM_STATE, SSM_GROUP
    ldt = jnp.broadcast_to(log_dt[:, None], (SSM_GROUPS, st))
    par = jnp.stack([a_re, a_im, ldt], axis=0).reshape(3, nb, gb, st)
    pe = jnp.broadcast_to(par[:, :, :, None, None, :], (3, nb, gb, ch, gb, st))
    pe = pe.reshape(3, nb, LANES, STATE_COLS).transpose(1, 0, 2, 3)
    bb = jnp.stack([b_re, b_im], axis=0).reshape(2, nb, gb, st, ch).transpose(0, 1, 2, 4, 3)
    bb = jnp.broadcast_to(bb[:, :, :, :, None, :], (2, nb, gb, ch, gb, st))
    bb = bb.reshape(2, nb, LANES, STATE_COLS).transpose(1, 0, 2, 3)
    pf = jnp.broadcast_to(par[:, :, :, :, None, None], (3, nb, gb, st, gb, ch))
    pf = pf.reshape(3, nb, STATE_COLS, LANES).transpose(1, 0, 2, 3)
    cc = jnp.stack([c_re, c_im], axis=0).reshape(2, nb, gb, ch, st).transpose(0, 1, 2, 4, 3)
    cc = jnp.broadcast_to(cc[:, :, :, :, None, :], (2, nb, gb, st, gb, ch))
    cc = cc.reshape(2, nb, STATE_COLS, LANES).transpose(1, 0, 2, 3)
    ps = par.reshape(3, nb, STATE_COLS).transpose(1, 0, 2)

    tl = S5_CHUNK * LANES
    blk4 = lambda i: (i, 0, 0, 0)
    blk3 = lambda i: (i, 0, 0)
    return pl.pallas_call(
        _s5_prep_kernel,
        grid=(nb,),
        in_specs=[
            pl.BlockSpec((1, 3, LANES, STATE_COLS), blk4),
            pl.BlockSpec((1, 2, LANES, STATE_COLS), blk4),
            pl.BlockSpec((1, 3, STATE_COLS, LANES), blk4),
            pl.BlockSpec((1, 2, STATE_COLS, LANES), blk4),
            pl.BlockSpec((1, 3, STATE_COLS), blk3),
        ],
        out_specs=(
            pl.BlockSpec((1, tl, 2 * STATE_COLS), blk3),
            pl.BlockSpec((1, 2 * STATE_COLS, tl), blk3),
            pl.BlockSpec((1, tl, tl), blk3),
            pl.BlockSpec((1, 2, 2 * STATE_COLS), blk3),
        ),
        out_shape=(
            jax.ShapeDtypeStruct((nb, tl, 2 * STATE_COLS), BF16),
            jax.ShapeDtypeStruct((nb, 2 * STATE_COLS, tl), BF16),
            jax.ShapeDtypeStruct((nb, tl, tl), BF16),
            jax.ShapeDtypeStruct((nb, 2, 2 * STATE_COLS), F32),
        ),
        scratch_shapes=[pltpu.VMEM((tl, 2 * STATE_COLS), F32),
                        pltpu.VMEM((2 * STATE_COLS, LANES), F32)],
        compiler_params=_cparams(("parallel",)),
        name="s5_prep",
    )(pe, bb, pf, cc, ps)


def _s5_prompt_kernel(u_ref, e_ref, m_ref, f_ref, pw_ref, d_ref,
                      y_ref, hl_ref, ub_ref, sloc_ref, hp_ref, hpb_ref):
    ct = pl.program_id(1)
    n_rows = u_ref.shape[1]
    n_batch = hl_ref.shape[1]
    per_batch = n_rows // n_batch
    sc = STATE_COLS

    @pl.when(ct == 0)
    def _():
        ub = u_ref[0].astype(BF16)
        ub_ref[...] = ub
        sloc_ref[...] = jnp.dot(ub, e_ref[0], preferred_element_type=F32)
        p_re = pw_ref[0, 1:2, 0:sc]
        p_im = pw_ref[0, 1:2, sc:2 * sc]

        def body(k, carry):
            new = []
            for b in range(n_batch):
                h_re, h_im = carry[2 * b], carry[2 * b + 1]
                row = b * per_batch + k
                hp_ref[pl.ds(row, 1), 0:sc] = h_re
                hp_ref[pl.ds(row, 1), sc:2 * sc] = h_im
                s_re = sloc_ref[pl.ds(row, 1), 0:sc]
                s_im = sloc_ref[pl.ds(row, 1), sc:2 * sc]
                new.append(p_re * h_re - p_im * h_im + s_re)
                new.append(p_re * h_im + p_im * h_re + s_im)
            return tuple(new)

        zero = jnp.zeros((1, sc), F32)
        final = lax.fori_loop(0, per_batch, body, (zero,) * (2 * n_batch))
        for b in range(n_batch):
            hl_ref[0, b:b + 1, 0:sc] = final[2 * b]
            hl_ref[0, b:b + 1, sc:2 * sc] = final[2 * b + 1]
        hpb_ref[...] = hp_ref[...].astype(BF16)

    w = y_ref.shape[2]
    col = pl.multiple_of(ct * w, w)
    y = jnp.dot(ub_ref[...], m_ref[0], preferred_element_type=F32)
    y = y + jnp.dot(hpb_ref[...], f_ref[0], preferred_element_type=F32)
    y_ref[0] = y + d_ref[0] * u_ref[0, :, pl.ds(col, w)]


def _s5_prompt(u_chunks, e_op, m_op, f_op, pw, d_tiled, n_batch):
    nb, n_rows, tl = u_chunks.shape
    w = S5_COL_TILE
    return pl.pallas_call(
        _s5_prompt_kernel,
        grid=(nb, tl // w),
        in_specs=[
            pl.BlockSpec((1, n_rows, tl), lambda i, c: (i, 0, 0)),
            pl.BlockSpec((1, tl, 2 * STATE_COLS), lambda i, c: (i, 0, 0)),
            pl.BlockSpec((1, tl, w), lambda i, c: (i, 0, c)),
            pl.BlockSpec((1, 2 * STATE_COLS, w), lambda i, c: (i, 0, c)),
            pl.BlockSpec((1, 2, 2 * STATE_COLS), lambda i, c: (i, 0, 0)),
            pl.BlockSpec((1, 1, w), lambda i, c: (i, 0, 0)),
        ],
        out_specs=(
            pl.BlockSpec((1, n_rows, w), lambda i, c: (i, 0, c)),
            pl.BlockSpec((1, n_batch, 2 * STATE_COLS), lambda i, c: (i, 0, 0)),
        ),
        out_shape=(
            jax.ShapeDtypeStruct((nb, n_rows, tl), F32),
            jax.ShapeDtypeStruct((nb, n_batch, 2 * STATE_COLS), F32),
        ),
        scratch_shapes=[pltpu.VMEM((n_rows, tl), BF16),
                        pltpu.VMEM((n_rows, 2 * STATE_COLS), F32),
                        pltpu.VMEM((n_rows, 2 * STATE_COLS), F32),
                        pltpu.VMEM((n_rows, 2 * STATE_COLS), BF16)],
        compiler_params=_cparams(("parallel", "arbitrary")),
        name="s5_prompt",
    )(u_chunks, e_op, m_op, f_op, pw, d_tiled)


def _s5_sample_kernel(u_ref, e_ref, m_ref, f_ref, pw_ref, d_ref, h0_ref, y_ref, hl_ref):
    sc = STATE_COLS
    u = u_ref[0]
    ub = u.astype(BF16)
    h0 = h0_ref[0]
    y = jnp.dot(ub, m_ref[0], preferred_element_type=F32)
    y = y + jnp.dot(h0.astype(BF16), f_ref[0], preferred_element_type=F32)
    y_ref[0] = y + d_ref[0] * u
    sloc = jnp.dot(ub, e_ref[0], preferred_element_type=F32)
    p_re = pw_ref[0, 0:1, 0:sc]
    p_im = pw_ref[0, 0:1, sc:2 * sc]
    h_re = h0[:, 0:sc]
    h_im = h0[:, sc:2 * sc]
    hl_ref[0, :, 0:sc] = p_re * h_re - p_im * h_im + sloc[:, 0:sc]
    hl_ref[0, :, sc:2 * sc] = p_re * h_im + p_im * h_re + sloc[:, sc:2 * sc]


def _s5_sample(u_chunks, e_op, m_op, f_op, pw, d_tiled, h0):
    nb, n_seq, w = u_chunks.shape
    tl = e_op.shape[1]
    e_blk = (tl - w) // w
    return pl.pallas_call(
        _s5_sample_kernel,
        grid=(nb,),
        in_specs=[
            pl.BlockSpec((1, n_seq, w), lambda i: (i, 0, 0)),
            pl.BlockSpec((1, w, 2 * STATE_COLS), lambda i: (i, e_blk, 0)),
            pl.BlockSpec((1, w, w), lambda i: (i, 0, 0)),
            pl.BlockSpec((1, 2 * STATE_COLS, w), lambda i: (i, 0, 0)),
            pl.BlockSpec((1, 2, 2 * STATE_COLS), lambda i: (i, 0, 0)),
            pl.BlockSpec((1, 1, w), lambda i: (i, 0, 0)),
            pl.BlockSpec((1, n_seq, 2 * STATE_COLS), lambda i: (i, 0, 0)),
        ],
        out_specs=(
            pl.BlockSpec((1, n_seq, w), lambda i: (i, 0, 0)),
            pl.BlockSpec((1, n_seq, 2 * STATE_COLS), lambda i: (i, 0, 0)),
        ),
        out_shape=(
            jax.ShapeDtypeStruct((nb, n_seq, w), F32),
            jax.ShapeDtypeStruct((nb, n_seq, 2 * STATE_COLS), F32),
        ),
        compiler_params=_cparams(("parallel",)),
        name="s5_sample",
    )(u_chunks, e_op, m_op, f_op, pw, d_tiled, h0)


def _mix_out_kernel(x_ref, attn_ref, y_ref, gw_ref, gb_ref, ga_ref, gs_ref, wo_ref, gp_ref,
                    o_ref, ycat_ref, mg_ref):
    for lb in range(N_LANE_BLOCKS):
        ycat_ref[:, lb * LANES:(lb + 1) * LANES] = jax.nn.gelu(y_ref[lb])
    y = ycat_ref[...]
    z = jnp.dot(y.astype(BF16), gw_ref[...], preferred_element_type=F32) + gb_ref[...]
    s = y * jax.nn.sigmoid(z)
    mg_ref[:, 0:D_ATTN] = _rms(attn_ref[...], ga_ref[...]).astype(BF16)
    mg_ref[:, D_ATTN:] = _rms(s, gs_ref[...]).astype(BF16)
    m = jnp.dot(mg_ref[...], wo_ref[...], preferred_element_type=F32)
    o_ref[...] = x_ref[...] + _rms(m, gp_ref[...])


def _mix_out(x, attn, y_blk, glu_w, glu_b, g_attn, g_ssm, w_out, g_post, layer):
    m = x.shape[0]
    tm = MIX_OUT_ROWS
    row = lambda i: (i, 0)
    const = lambda i: (0, 0)
    per_layer = lambda i: (layer, 0, 0)
    return pl.pallas_call(
        _mix_out_kernel,
        grid=(m // tm,),
        in_specs=[
            pl.BlockSpec((tm, D_MODEL), row),
            pl.BlockSpec((tm, D_ATTN), row),
            pl.BlockSpec((N_LANE_BLOCKS, tm, LANES), lambda i: (0, i, 0)),
            pl.BlockSpec((None, D_SSM, D_SSM), per_layer),
            pl.BlockSpec((1, D_SSM), const),
            pl.BlockSpec((1, D_ATTN), const),
            pl.BlockSpec((1, D_SSM), const),
            pl.BlockSpec((None, D_MODEL, D_MODEL), per_layer),
            pl.BlockSpec((1, D_MODEL), const),
        ],
        out_specs=pl.BlockSpec((tm, D_MODEL), row),
        out_shape=jax.ShapeDtypeStruct((m, D_MODEL), F32),
        scratch_shapes=[pltpu.VMEM((tm, D_SSM), F32), pltpu.VMEM((tm, D_MODEL), BF16)],
        compiler_params=_cparams(("parallel",)),
        name="mix_out",
    )(x, attn, y_blk, glu_w, glu_b, g_attn, g_ssm, w_out, g_post)


def _state_to_blocks(h):
    n = h.shape[0]
    h = h.reshape(n, N_LANE_BLOCKS, STATE_COLS, 2)
    return h.transpose(1, 0, 3, 2).reshape(N_LANE_BLOCKS, n, 2 * STATE_COLS)


def _blocks_to_state(h):
    n = h.shape[1]
    h = h.reshape(N_LANE_BLOCKS, n, 2, STATE_COLS).transpose(1, 0, 3, 2)
    return h.reshape(n, SSM_GROUPS, SSM_STATE, 2)


def _project_in(x, l, w):
    u_blk, q, k, v, kb, vb, lf = _mix_in(x, w["ng"][l][2], w["w_in"], w["w_f"], w["b_f"], l)
    return u_blk, q, k, v, kb, vb, lf[:, :N_HEADS]


def _project_out(x, attn, y_blk, l, w):
    return _mix_out(x, attn, y_blk, w["glu_w"], w["glu_b"][l], w["g_attn"][l], w["g_ssm"][l],
                    w["w_out"], w["ng"][l][3], l)


def kernel(x_prompt, x_sample, cache_k, cache_v, cache_logf, state_ssm, page_table, norm_g, ffn_gate, ffn_up, ffn_down, w_in, b_forget, ssm_a_re, ssm_a_im, ssm_log_dt, ssm_b_re, ssm_b_im, ssm_c_re, ssm_c_im, ssm_d, glu_w, glu_b, g_attn_out, g_ssm_out, w_out):
    batch, seq, _ = x_prompt.shape
    n_seq, n_new, _ = x_sample.shape
    depth = norm_g.shape[0]
    n_pool, page = cache_k.shape[1], cache_k.shape[2]
    assert seq % ROW_TILE == 0 and seq % ATT_TILE == 0 and seq % S5_CHUNK == 0
    assert (n_seq * n_new) % 256 == 0 and page_table.shape[1] == PAGES_PER_SEQ
    assert n_pool % DECAY_PAGES == 0 and N_HEADS == SUBLANES and n_new == DECODE_CHUNK

    xp = x_prompt.reshape(batch * seq, D_MODEL)
    xs = x_sample.reshape(n_seq * n_new, D_MODEL)

    n_main = D_SSM + 3 * D_ATTN
    w = {
        "ng": [[norm_g[l, i].reshape(1, D_MODEL) for i in range(6)] for l in range(depth)],
        "wg": ffn_gate.astype(BF16),
        "wu": ffn_up.astype(BF16),
        "wd": ffn_down.astype(BF16),
        "w_in": w_in.astype(BF16),
        "w_f": jnp.pad(w_in[:, :, n_main:], ((0, 0), (0, 0), (0, LANES - N_HEADS))).astype(BF16),
        "b_f": jnp.pad(b_forget, ((0, 0), (0, LANES - N_HEADS))).reshape(depth, 1, LANES),
        "glu_w": glu_w.astype(BF16),
        "glu_b": [glu_b[l].reshape(1, D_SSM) for l in range(depth)],
        "g_attn": [g_attn_out[l].reshape(1, D_ATTN) for l in range(depth)],
        "g_ssm": [g_ssm_out[l].reshape(1, D_SSM) for l in range(depth)],
        "w_out": w_out.astype(BF16),
    }

    suffix, totals = _page_decay(cache_logf.transpose(0, 1, 3, 2))
    decay = (suffix.transpose(0, 1, 3, 2).reshape(depth, n_pool, 1, page * N_HEADS), totals)

    outs = {name: [] for name in ("kp", "vp", "lfp", "sp", "ks", "vs", "lfs", "ss")}
    for l in range(depth):
        e_op, f_op, m_op, pw = _s5_prep(ssm_a_re[l], ssm_a_im[l], ssm_log_dt[l],
                                        ssm_b_re[l], ssm_b_im[l], ssm_c_re[l], ssm_c_im[l])
        d_blk = ssm_d[l].reshape(N_LANE_BLOCKS, 1, LANES)
        d_prompt = jnp.tile(d_blk, (1, 1, S5_COL_TILE // LANES))
        d_sample = jnp.tile(d_blk, (1, 1, n_new))

        ng = w["ng"][l]
        ffn_w = (w["wg"], w["wu"], w["wd"])
        rows = n_new * N_HEADS

        xp = _ffn_half(xp, ng[0], ng[1], *ffn_w, l, 0)
        u1, q1, k1, v1, kb1, vb1, lf1 = _project_in(xp, l, w)
        lf_t = lf1.reshape(batch, seq, N_HEADS).transpose(0, 2, 1)
        c_col = _cumsum_heads(lf_t).transpose(0, 2, 1)
        attn1 = _flash_prompt(q1.reshape(batch, seq, D_ATTN), kb1.reshape(batch, seq, D_ATTN),
                              vb1.reshape(batch, seq, D_ATTN), c_col)
        u_chunks = u1.reshape(N_LANE_BLOCKS, batch * seq // S5_CHUNK, S5_CHUNK * LANES)
        y1, h1 = _s5_prompt(u_chunks, e_op, m_op, f_op, pw, d_prompt, batch)
        s1 = _blocks_to_state(h1)
        xp = _project_out(xp, attn1.reshape(batch * seq, D_ATTN),
                          y1.reshape(N_LANE_BLOCKS, batch * seq, LANES), l, w)
        xp = _ffn_half(xp, ng[4], ng[5], *ffn_w, l, 1)

        xs = _ffn_half(xs, ng[0], ng[1], *ffn_w, l, 0)
        u2, q2, k2, v2, _, _, lf2 = _project_in(xs, l, w)
        attn2 = _sample_attn(q2.astype(F32).reshape(n_seq, rows, HEAD_DIM),
                             k2.reshape(n_seq, rows, HEAD_DIM), v2.reshape(n_seq, rows, HEAD_DIM),
                             lf2.reshape(n_seq, rows), cache_k, cache_v, decay, page_table, l)
        attn2 = attn2.reshape(n_seq * n_new, D_ATTN)
        y2, h2 = _s5_sample(u2.reshape(N_LANE_BLOCKS, n_seq, n_new * LANES), e_op, m_op, f_op, pw,
                            d_sample, _state_to_blocks(state_ssm[l]))
        s2 = _blocks_to_state(h2)
        xs = _project_out(xs, attn2, y2.reshape(N_LANE_BLOCKS, n_seq * n_new, LANES), l, w)
        xs = _ffn_half(xs, ng[4], ng[5], *ffn_w, l, 1)
        outs["kp"].append(k1.reshape(batch, seq, N_HEADS, HEAD_DIM))
        outs["vp"].append(v1.reshape(batch, seq, N_HEADS, HEAD_DIM))
        outs["lfp"].append(lf1.reshape(batch, seq, N_HEADS))
        outs["sp"].append(s1)
        outs["ks"].append(k2.reshape(n_seq, n_new, N_HEADS, HEAD_DIM))
        outs["vs"].append(v2.reshape(n_seq, n_new, N_HEADS, HEAD_DIM))
        outs["lfs"].append(lf2.reshape(n_seq, n_new, N_HEADS))
        outs["ss"].append(s2)

    return (xp.reshape(batch, seq, D_MODEL), xs.reshape(n_seq, n_new, D_MODEL),
            jnp.stack(outs["kp"]), jnp.stack(outs["vp"]), jnp.stack(outs["lfp"]),
            jnp.stack(outs["sp"]),
            jnp.stack(outs["ks"]), jnp.stack(outs["vs"]), jnp.stack(outs["lfs"]),
            jnp.stack(outs["ss"]))
```
